```python
import jax
import jax.numpy as jnp
from jax import lax
import numpy as np

D_MODEL = 4096
BATCH = 32
SEQ = 256
DEPTH = 2
DEC_BATCH = 4
DEC_SEQ = 2048
PAST_LEN = 256

GRID_W = 64
D_INNER = D_MODEL
N_EVEN = (DEPTH + 1) // 2
N_ODD = DEPTH // 2
D_FOURIER = D_INNER // 2
FOURIER_GROUPS = 4
FOURIER_CG = D_FOURIER // FOURIER_GROUPS
D_POOL = D_INNER - D_FOURIER
POOL_WINDOWS = (2, 4, 8, 16)
POOL_GROUPS = len(POOL_WINDOWS)
POOL_CG = D_POOL // POOL_GROUPS
EVEN_IN = 2 * D_INNER
D_ATTN = D_INNER // 2
HEAD_DIM = 128
N_HEADS = D_ATTN // HEAD_DIM
D_CONV = D_INNER - D_ATTN
CONV_WIDTH = 31
NA_KH = 8
NA_KW = 16
NA_QC = 16
NA_KSPAN = 2 * NA_KW
ODD_IN = 3 * D_ATTN + 2 * D_CONV + D_INNER
EPS = 1e-6
NEG_INF = -1e30

kernel_name = "hybrid_flow_backbone_step"


def rmsnorm(x, g):
    xf = x.astype(jnp.float32)
    y = xf * lax.rsqrt(jnp.mean(xf * xf, axis=-1, keepdims=True) + EPS)
    return (y * g.astype(jnp.float32)).astype(x.dtype)


def layernorm(x, g, b):
    xf = x.astype(jnp.float32)
    mu = jnp.mean(xf, axis=-1, keepdims=True)
    xc = xf - mu
    y = xc * lax.rsqrt(jnp.mean(xc * xc, axis=-1, keepdims=True) + EPS)
    return (y * g.astype(jnp.float32) + b.astype(jnp.float32)).astype(x.dtype)


def adaln(cond, w, b):
    mod = (jax.nn.silu(cond) @ w + b)[:, None, :]
    return jnp.split(mod, 3, axis=-1)


def fourier_mixer(u, w):
    B, L, _ = u.shape
    ug = u.reshape(B, L, FOURIER_GROUPS, FOURIER_CG).astype(jnp.float32)
    f = jnp.fft.fftn(ug, axes=(1, 3), norm="ortho").real.astype(u.dtype)
    return jnp.einsum("blgc,gcd->blgd", f, w).reshape(B, L, D_FOURIER)


def pool_mixer(u, w, scale):
    B, L, _ = u.shape
    ug = u.reshape(B, L, POOL_GROUPS, POOL_CG)
    csum = jnp.pad(jnp.cumsum(ug.astype(jnp.float32), axis=1), ((0, 0), (1, 0), (0, 0), (0, 0)))
    t = np.arange(L)
    means = []
    for g, win in enumerate(POOL_WINDOWS):
        lo = np.maximum(t - win // 2, 0)
        hi = np.minimum(t + win // 2, L)
        cnt = (hi - lo).astype(np.float32)
        means.append((csum[:, hi, g] - csum[:, lo, g]) / cnt[None, :, None])
    d = (jnp.stack(means, axis=2) - ug.astype(jnp.float32)).astype(u.dtype)
    y = jnp.einsum("blgc,gcd->blgd", d, w).reshape(B, L, D_POOL)
    return y * scale


def conv_module(u, dw, dw_b, ln_g, ln_b, w_pw):
    a, b = jnp.split(u, 2, axis=-1)
    h = a * jax.nn.sigmoid(b)
    h = lax.conv_general_dilated(h, dw[:, None, :], (1,), [(CONV_WIDTH // 2, CONV_WIDTH // 2)],
                                 dimension_numbers=("NWC", "WIO", "NWC"),
                                 feature_group_count=D_CONV) + dw_b
    h = jax.nn.silu(layernorm(h, ln_g, ln_b))
    return h @ w_pw


def split_heads(u):
    B, L, _ = u.shape
    return u.reshape(B, L, N_HEADS, HEAD_DIM).transpose(0, 2, 1, 3)


def merge_heads(o):
    B, H, L, HD = o.shape
    return o.transpose(0, 2, 1, 3).reshape(B, L, H * HD)


def context_attention(q, k, v):
    s = jnp.einsum("bhqd,bhkd->bhqk", q, k).astype(jnp.float32) * HEAD_DIM ** -0.5
    p = jax.nn.softmax(s, axis=-1).astype(v.dtype)
    return jnp.einsum("bhqk,bhkd->bhqd", p, v)


def neighbourhood_attention(q, k, v, k_ctx, v_ctx, rel_bias):
    B, H, L, HD = q.shape
    rows = L // GRID_W
    kh = min(NA_KH, rows)
    nb = GRID_W // NA_QC
    qg = q.reshape(B, H, rows, GRID_W, HD)
    kg = k.reshape(B, H, rows, GRID_W, HD)
    vg = v.reshape(B, H, rows, GRID_W, HD)
    q_cols = np.arange(GRID_W).reshape(nb, NA_QC)
    q_start = np.clip(q_cols - NA_KW // 2, 0, GRID_W - NA_KW)
    blk_start = np.minimum(q_start[:, 0], GRID_W - NA_KSPAN)
    key_cols = blk_start[:, None] + np.arange(NA_KSPAN)[None, :]
    kc = key_cols[:, None, :]
    col_ok = (kc >= q_start[..., None]) & (kc < q_start[..., None] + NA_KW)
    dc_idx = np.clip(kc - q_cols[..., None] + NA_KW - 1, 0, 2 * NA_KW - 2)
    bias_c = rel_bias[:, :, dc_idx].astype(jnp.float32)
    scale = HEAD_DIM ** -0.5
    n_loc = kh * NA_KSPAN

    def one_row(r):
        start = jnp.clip(r - kh // 2, 0, rows - kh)
        qr = lax.dynamic_index_in_dim(qg, r, axis=2, keepdims=False).reshape(B, H, nb, NA_QC, HD)
        kr = lax.dynamic_slice_in_dim(kg, start, kh, axis=2)[:, :, :, key_cols]
        vr = lax.dynamic_slice_in_dim(vg, start, kh, axis=2)[:, :, :, key_cols]
        dr_idx = start + jnp.arange(kh) - r + NA_KH - 1
        bias = jnp.take(bias_c, dr_idx, axis=1).transpose(0, 2, 3, 1, 4)
        s_loc = jnp.einsum("bhnqd,bhrnkd->bhnqrk", qr, kr).astype(jnp.float32) * scale + bias[None]
        s_loc = jnp.where(col_ok[:, :, None, :], s_loc, NEG_INF).reshape(B, H, nb, NA_QC, n_loc)
        s_ctx = jnp.einsum("bhnqd,bhkd->bhnqk", qr, k_ctx).astype(jnp.float32) * scale
        p = jax.nn.softmax(jnp.concatenate([s_loc, s_ctx], axis=-1), axis=-1).astype(v.dtype)
        p_loc = p[..., :n_loc].reshape(B, H, nb, NA_QC, kh, NA_KSPAN)
        p_ctx = p[..., n_loc:]
        o = (jnp.einsum("bhnqrk,bhrnkd->bhnqd", p_loc, vr)
             + jnp.einsum("bhnqk,bhkd->bhnqd", p_ctx, v_ctx))
        return o.reshape(B, H, GRID_W, HD)

    out = lax.map(one_row, jnp.arange(rows))
    return out.transpose(1, 2, 0, 3, 4).reshape(B, H, L, HD)


def run_trunk(x, cond, cache_k, cache_v, norm_g, w_ada, b_ada, w_in_even, w_out_even, w_fourier,
              w_pool, pool_scale, w_in_odd, w_out_odd, q_norm_g, k_norm_g, rel_bias, conv_dw,
              conv_dw_b, conv_ln_g, conv_ln_b, w_conv_pw):
    is_context = cache_k is None
    new_k, new_v = [], []
    for i in range(DEPTH):
        shift, scale, gate = adaln(cond, w_ada[i], b_ada[i])
        h = rmsnorm(x, norm_g[i]) * (1 + scale) + shift
        j = i // 2
        if i % 2 == 0:
            u = h @ w_in_even[j]
            a_in, b_in, gp = jnp.split(u, [D_FOURIER, D_INNER], axis=-1)
            y = jnp.concatenate([fourier_mixer(a_in, w_fourier[j]),
                                 pool_mixer(b_in, w_pool[j], pool_scale[j])], axis=-1)
            y = (y * jax.nn.silu(gp)) @ w_out_even[j]
        else:
            u = h @ w_in_odd[j]
            q_in, k_in, v_in, c_in, gp = jnp.split(
                u, [D_ATTN, 2 * D_ATTN, 3 * D_ATTN, 3 * D_ATTN + 2 * D_CONV], axis=-1)
            q = rmsnorm(split_heads(q_in), q_norm_g[j])
            k = rmsnorm(split_heads(k_in), k_norm_g[j])
            v = split_heads(v_in)
            if is_context:
                o = context_attention(q, k, v)
                new_k.append(k)
                new_v.append(v)
            else:
                o = neighbourhood_attention(q, k, v, cache_k[:, j], cache_v[:, j], rel_bias[j])
            yd = conv_module(c_in, conv_dw[j], conv_dw_b[j], conv_ln_g[j], conv_ln_b[j], w_conv_pw[j])
            y = jnp.concatenate([merge_heads(o), yd], axis=-1)
            y = (y * jax.nn.silu(gp)) @ w_out_odd[j]
        x = x + gate * y
    return x, new_k, new_v


def setup_inputs(seed: int = 0) -> dict:
    key = jax.random.key(seed)
    ks = jax.random.split(key, 24)

    def nrm(k, shape, s):
        return jax.random.normal(k, shape, jnp.float32) * s

    return {
        "x_prompt": nrm(ks[0], (BATCH, SEQ, D_MODEL), 1.0),
        "x_sample": nrm(ks[1], (DEC_BATCH, DEC_SEQ, D_MODEL), 1.0),
        "cache_k": nrm(ks[2], (DEC_BATCH, N_ODD, N_HEADS, PAST_LEN, HEAD_DIM), 1.0),
        "cache_v": nrm(ks[3], (DEC_BATCH, N_ODD, N_HEADS, PAST_LEN, HEAD_DIM), 1.0),
        "c": nrm(ks[4], (DEC_BATCH, D_MODEL), 1.0),
        "c_ctx": nrm(ks[5], (D_MODEL,), 1.0),
        "norm_g": 1.0 + nrm(ks[6], (DEPTH, D_MODEL), 0.05),
        "w_ada": nrm(ks[7], (DEPTH, D_MODEL, 3 * D_MODEL), 0.5 * D_MODEL ** -0.5),
        "b_ada": nrm(ks[8], (DEPTH, 3 * D_MODEL), 0.02),
        "w_in_even": nrm(ks[9], (N_EVEN, D_MODEL, EVEN_IN), D_MODEL ** -0.5),
        "w_out_even": nrm(ks[10], (N_EVEN, D_INNER, D_MODEL), D_INNER ** -0.5),
        "w_fourier": nrm(ks[11], (N_EVEN, FOURIER_GROUPS, FOURIER_CG, FOURIER_CG), FOURIER_CG ** -0.5),
        "w_pool": nrm(ks[12], (N_EVEN, POOL_GROUPS, POOL_CG, POOL_CG), POOL_CG ** -0.5),
        "pool_scale": 1.0 + nrm(ks[13], (N_EVEN, D_POOL), 0.05),
        "w_in_odd": nrm(ks[14], (N_ODD, D_MODEL, ODD_IN), D_MODEL ** -0.5),
        "w_out_odd": nrm(ks[15], (N_ODD, D_INNER, D_MODEL), D_INNER ** -0.5),
        "q_norm_g": 1.0 + nrm(ks[16], (N_ODD, HEAD_DIM), 0.05),
        "k_norm_g": 1.0 + nrm(ks[17], (N_ODD, HEAD_DIM), 0.05),
        "rel_bias": nrm(ks[18], (N_ODD, N_HEADS, 2 * NA_KH - 1, 2 * NA_KW - 1), 0.1),
        "conv_dw": nrm(ks[19], (N_ODD, CONV_WIDTH, D_CONV), CONV_WIDTH ** -0.5),
        "conv_dw_b": nrm(ks[20], (N_ODD, D_CONV), 0.02),
        "conv_ln_g": 1.0 + nrm(ks[21], (N_ODD, D_CONV), 0.05),
        "conv_ln_b": nrm(ks[22], (N_ODD, D_CONV), 0.02),
        "w_conv_pw": nrm(ks[23], (N_ODD, D_CONV, D_CONV), D_CONV ** -0.5),
    }


def reference(x_prompt, x_sample, cache_k, cache_v, c, c_ctx, norm_g, w_ada, b_ada, w_in_even,
              w_out_even, w_fourier, w_pool, pool_scale, w_in_odd, w_out_odd, q_norm_g, k_norm_g,
              rel_bias, conv_dw, conv_dw_b, conv_ln_g, conv_ln_b, w_conv_pw):
    weights = (norm_g, w_ada, b_ada, w_in_even, w_out_even, w_fourier, w_pool, pool_scale,
               w_in_odd, w_out_odd, q_norm_g, k_norm_g, rel_bias, conv_dw, conv_dw_b,
               conv_ln_g, conv_ln_b, w_conv_pw)
    y_prompt, ks_new, vs_new = run_trunk(x_prompt, c_ctx[None, :], None, None, *weights)
    state_k = jnp.stack(ks_new, axis=1)
    state_v = jnp.stack(vs_new, axis=1)
    y_sample, _, _ = run_trunk(x_sample, c, cache_k, cache_v, *weights)
    return (y_prompt, y_sample, state_k, state_v)
```

```python
import functools

import numpy as np
import jax
import jax.numpy as jnp
from jax import lax
from jax.experimental import pallas as pl
from jax.experimental.pallas import tpu as pltpu

F32 = jnp.float32
BF16 = jnp.bfloat16

GRID_W = 64
POOL_WINDOWS = (2, 4, 8, 16)
EPS = 1e-6
NEG_INF = -1e30

LANES = 128
V7X_VMEM_LIMIT_BYTES = 56 * 1024 * 1024

NA_QROWS = 8
NA_KROWS = 16
POOL_TILE = 256
POOL_HALO = 128
CONV_TILE = 256
CONV_HALO = 16
CONV_ROWS = 64


def _cparams(*sem):
    return pltpu.CompilerParams(dimension_semantics=sem, vmem_limit_bytes=V7X_VMEM_LIMIT_BYTES)


def _pick(n, pref, unit=LANES):
    if n <= pref:
        return n
    t = (pref // unit) * unit
    while t > unit and n % t:
        t -= unit
    assert n % t == 0, (n, pref, unit)
    return t


def _silu(x):
    return x * jax.nn.sigmoid(x)


def _rms(x, g):
    return x * lax.rsqrt(jnp.mean(x * x, axis=-1, keepdims=True) + EPS) * g


def _adaln_kernel(cond_ref, w_ref, b_ref, o_ref):
    s = _silu(cond_ref[...]).astype(BF16)
    o_ref[0] = jnp.dot(s, w_ref[0].astype(BF16), preferred_element_type=F32) + b_ref[0]


def _adaln(cond, w_ada, b_ada):
    depth, d, n = w_ada.shape
    r = cond.shape[0]
    tn = _pick(n, 512)
    return pl.pallas_call(
        _adaln_kernel,
        grid=(depth, n // tn),
        in_specs=[pl.BlockSpec((r, d), lambda l, j: (0, 0)),
                  pl.BlockSpec((1, d, tn), lambda l, j: (l, 0, j)),
                  pl.BlockSpec((1, 1, tn), lambda l, j: (l, 0, j))],
        out_specs=pl.BlockSpec((1, r, tn), lambda l, j: (l, 0, j)),
        out_shape=jax.ShapeDtypeStruct((depth, r, n), F32),
        compiler_params=_cparams("parallel", "parallel"),
    )(cond, w_ada, b_ada.reshape(depth, 1, n))


def _norm_mod_kernel(x_ref, g_ref, shift_ref, scale_ref, o_ref):
    y = _rms(x_ref[...], g_ref[...])
    o_ref[...] = (y * (1.0 + scale_ref[0]) + shift_ref[0]).astype(o_ref.dtype)


def _norm_mod(x, g, mod3, row_of):
    t, d = x.shape
    tm = _pick(t, 512, 8)
    return pl.pallas_call(
        _norm_mod_kernel,
        grid=(t // tm,),
        in_specs=[pl.BlockSpec((tm, d), lambda i: (i, 0)),
                  pl.BlockSpec((1, d), lambda i: (0, 0)),
                  pl.BlockSpec((1, 1, d), lambda i: (row_of(i * tm), 0, 0)),
                  pl.BlockSpec((1, 1, d), lambda i: (row_of(i * tm), 0, 1))],
        out_specs=pl.BlockSpec((tm, d), lambda i: (i, 0)),
        out_shape=jax.ShapeDtypeStruct((t, d), BF16),
        compiler_params=_cparams("parallel"),
    )(x, g.reshape(1, d), mod3, mod3)


def _matmul_kernel(a_ref, w_ref, o_ref):
    o_ref[...] = jnp.dot(a_ref[...], w_ref[...], preferred_element_type=F32).astype(o_ref.dtype)


def _matmul(a, w, out_dtype=BF16):
    m, k = a.shape
    n = w.shape[1]
    bm, bn = _pick(m, 1024, 8), _pick(n, 1024)
    return pl.pallas_call(
        _matmul_kernel,
        grid=(m // bm, n // bn),
        in_specs=[pl.BlockSpec((bm, k), lambda i, j: (i, 0)),
                  pl.BlockSpec((k, bn), lambda i, j: (0, j))],
        out_specs=pl.BlockSpec((bm, bn), lambda i, j: (i, j)),
        out_shape=jax.ShapeDtypeStruct((m, n), out_dtype),
        compiler_params=_cparams("parallel", "parallel"),
    )(a, w)


def _out_proj_kernel(a1_ref, a2_ref, w1_ref, w2_ref, x_ref, gate_ref, o_ref):
    acc = jnp.dot(a1_ref[...], w1_ref[...], preferred_element_type=F32)
    acc = acc + jnp.dot(a2_ref[...], w2_ref[...], preferred_element_type=F32)
    o_ref[...] = x_ref[...] + gate_ref[0] * acc


def _out_proj(a1, a2, w, x, mod3, row_of):
    t, k1 = a1.shape
    d = w.shape[1]
    assert a2.shape == (t, k1) and w.shape[0] == 2 * k1
    bm, bn = _pick(t, 1024, 8), _pick(d, 512)
    gate_blk = 2 * (d // bn)
    return pl.pallas_call(
        _out_proj_kernel,
        grid=(t // bm, d // bn),
        in_specs=[pl.BlockSpec((bm, k1), lambda i, j: (i, 0)),
                  pl.BlockSpec((bm, k1), lambda i, j: (i, 0)),
                  pl.BlockSpec((k1, bn), lambda i, j: (0, j)),
                  pl.BlockSpec((k1, bn), lambda i, j: (1, j)),
                  pl.BlockSpec((bm, bn), lambda i, j: (i, j)),
                  pl.BlockSpec((1, 1, bn), lambda i, j: (row_of(i * bm), 0, gate_blk + j))],
        out_specs=pl.BlockSpec((bm, bn), lambda i, j: (i, j)),
        out_shape=jax.ShapeDtypeStruct((t, d), F32),
        compiler_params=_cparams("parallel", "parallel"),
    )(a1, a2, w, w, x, mod3)


def _dft_cos_sin(n):
    idx = np.arange(n, dtype=np.int64)
    ang = 2.0 * np.pi * ((idx[:, None] * idx[None, :]) % n) / n
    return np.cos(ang) / np.sqrt(n), np.sin(ang) / np.sqrt(n)


def _fourier_kernel(u_ref, gp_ref, csc_ref, csl_ref, wf_ref, o_ref, p_scr, *, seq, cg):
    @pl.when(pl.program_id(2) == 0)
    def _():
        p = jnp.dot(u_ref[...], csc_ref[...], preferred_element_type=F32)
        p_scr[0:seq, :] = p[:, :cg].astype(BF16)
        p_scr[seq:2 * seq, :] = p[:, cg:].astype(BF16)

    f = jnp.dot(csl_ref[...], p_scr[...], preferred_element_type=F32)
    y = jnp.dot(f.astype(BF16), wf_ref[0], preferred_element_type=F32)
    o_ref[...] = (y * _silu(gp_ref[...].astype(F32))).astype(o_ref.dtype)


def _fourier(u, wf, nb, seq, gp_col0):
    groups, cg, _ = wf.shape
    t = u.shape[0]
    tr = _pick(seq, 512, 8)
    rt = seq // tr
    cc, sc = _dft_cos_sin(cg)
    cl, sl = _dft_cos_sin(seq)
    csc = jnp.asarray(np.concatenate([cc, sc], axis=1), BF16)
    csl = jnp.asarray(np.concatenate([cl, -sl], axis=1), BF16)
    gp_blk0 = gp_col0 // cg
    return pl.pallas_call(
        functools.partial(_fourier_kernel, seq=seq, cg=cg),
        grid=(nb, groups, rt),
        in_specs=[pl.BlockSpec((seq, cg), lambda b, g, r: (b, g)),
                  pl.BlockSpec((tr, cg), lambda b, g, r: (b * rt + r, gp_blk0 + g)),
                  pl.BlockSpec((cg, 2 * cg), lambda b, g, r: (0, 0)),
                  pl.BlockSpec((tr, 2 * seq), lambda b, g, r: (r, 0)),
                  pl.BlockSpec((1, cg, cg), lambda b, g, r: (g, 0, 0))],
        out_specs=pl.BlockSpec((tr, cg), lambda b, g, r: (b * rt + r, g)),
        out_shape=jax.ShapeDtypeStruct((t, groups * cg), BF16),
        scratch_shapes=[pltpu.VMEM((2 * seq, cg), BF16)],
        compiler_params=_cparams("parallel", "parallel", "arbitrary"),
    )(u, u, csc, csl, wf)


def _pool_tables(seq):
    r = np.arange(POOL_TILE)[:, None]
    s = np.arange(POOL_TILE + 2 * POOL_HALO)[None, :] - POOL_HALO
    tpos = np.arange(seq)
    bands, cnts = [], []
    for win in POOL_WINDOWS:
        half = win // 2
        assert half <= POOL_HALO
        bands.append(((s >= r - half) & (s <= r + half - 1)).astype(np.float32))
        cnts.append(np.minimum(tpos + half, seq) - np.maximum(tpos - half, 0))
    return np.stack(bands), np.stack(cnts).astype(np.float32)[:, :, None]


def _pool_kernel(u_ref, gp_ref, band_ref, cnt_ref, w_ref, ps_ref, o_ref, *, seq):
    tr = POOL_TILE if seq >= POOL_TILE else seq
    for t in range(seq // tr):
        base = tr * t - POOL_HALO
        lo, hi = max(base, 0), min(base + tr + 2 * POOL_HALO, seq)
        rows = slice(tr * t, tr * (t + 1))
        wsum = jnp.dot(band_ref[0, :, lo - base:hi - base], u_ref[lo:hi, :],
                       preferred_element_type=F32)
        d = (wsum / cnt_ref[0, rows, :] - u_ref[rows, :].astype(F32)).astype(BF16)
        y = jnp.dot(d, w_ref[0], preferred_element_type=F32) * ps_ref[...]
        o_ref[rows, :] = (y * _silu(gp_ref[rows, :].astype(F32))).astype(o_ref.dtype)


def _pool(u, wp, pool_scale, nb, seq, in_col0, gp_col0):
    groups, cg, _ = wp.shape
    assert groups == len(POOL_WINDOWS) and seq % POOL_TILE == 0
    t = u.shape[0]
    band, cnt = _pool_tables(seq)
    in_blk0, gp_blk0 = in_col0 // cg, gp_col0 // cg
    return pl.pallas_call(
        functools.partial(_pool_kernel, seq=seq),
        grid=(nb, groups),
        in_specs=[pl.BlockSpec((seq, cg), lambda b, g: (b, in_blk0 + g)),
                  pl.BlockSpec((seq, cg), lambda b, g: (b, gp_blk0 + g)),
                  pl.BlockSpec((1,) + band.shape[1:], lambda b, g: (g, 0, 0)),
                  pl.BlockSpec((1, seq, 1), lambda b, g: (g, 0, 0)),
                  pl.BlockSpec((1, cg, cg), lambda b, g: (g, 0, 0)),
                  pl.BlockSpec((1, cg), lambda b, g: (0, g))],
        out_specs=pl.BlockSpec((seq, cg), lambda b, g: (b, g)),
        out_shape=jax.ShapeDtypeStruct((t, groups * cg), BF16),
        compiler_params=_cparams("parallel", "parallel"),
    )(u, u, jnp.asarray(band, BF16), jnp.asarray(cnt, F32), wp, pool_scale.reshape(1, groups * cg))


def _ctx_attn_kernel(q_ref, k_ref, v_ref, gp_ref, qg_ref, kg_ref, o_ref, sk_ref, sv_ref, *, heads, hd):
    scale = hd ** -0.5
    for h in range(heads):
        cols = slice(h * hd, (h + 1) * hd)
        qn = _rms(q_ref[:, cols].astype(F32), qg_ref[...])
        kn = _rms(k_ref[:, cols].astype(F32), kg_ref[...])
        v = v_ref[:, cols]
        sk_ref[0, 0, h] = kn
        sv_ref[0, 0, h] = v.astype(F32)
        s = lax.dot_general(qn.astype(BF16), kn.astype(BF16), (((1,), (1,)), ((), ())),
                            preferred_element_type=F32) * scale
        p = jnp.exp(s - jnp.max(s, axis=-1, keepdims=True))
        o = jnp.dot(p.astype(BF16), v, preferred_element_type=F32) / jnp.sum(p, axis=-1, keepdims=True)
        o_ref[:, cols] = (o * _silu(gp_ref[:, cols].astype(F32))).astype(o_ref.dtype)


def _ctx_attn(u, qg, kg, nb, seq, d_attn, gp_col0):
    hd = qg.shape[-1]
    heads = d_attn // hd
    t = u.shape[0]
    state = jax.ShapeDtypeStruct((nb, 1, heads, seq, hd), F32)
    state_spec = pl.BlockSpec((1, 1, heads, seq, hd), lambda b: (b, 0, 0, 0, 0))
    gp_blk = gp_col0 // d_attn
    return pl.pallas_call(
        functools.partial(_ctx_attn_kernel, heads=heads, hd=hd),
        grid=(nb,),
        in_specs=[pl.BlockSpec((seq, d_attn), lambda b: (b, 0)),
                  pl.BlockSpec((seq, d_attn), lambda b: (b, 1)),
                  pl.BlockSpec((seq, d_attn), lambda b: (b, 2)),
                  pl.BlockSpec((seq, d_attn), lambda b: (b, gp_blk)),
                  pl.BlockSpec((1, hd), lambda b: (0, 0)),
                  pl.BlockSpec((1, hd), lambda b: (0, 0))],
        out_specs=[pl.BlockSpec((seq, d_attn), lambda b: (b, 0)), state_spec, state_spec],
        out_shape=[jax.ShapeDtypeStruct((t, d_attn), BF16), state, state],
        compiler_params=_cparams("parallel"),
    )(u, u, u, u, qg.reshape(1, hd), kg.reshape(1, hd))


def _nbr_plan(rows, kh, kw):
    assert rows % NA_QROWS == 0 and rows >= NA_KROWS
    n_dr = 2 * kh - 1
    blocks, types = [], []
    for r0 in range(0, rows, NA_QROWS):
        ws = int(np.clip(r0 - kh // 2, 0, rows - NA_KROWS))
        idx = np.full((NA_QROWS, NA_KROWS), n_dr, np.int32)
        for i in range(NA_QROWS):
            r = r0 + i
            start = int(np.clip(r - kh // 2, 0, rows - kh))
            assert ws <= start and start + kh <= ws + NA_KROWS
            for j in range(NA_KROWS):
                rk = ws + j
                if start <= rk < start + kh:
                    idx[i, j] = rk - r + kh - 1
        for t, known in enumerate(types):
            if np.array_equal(known, idx):
                break
        else:
            t = len(types)
            types.append(idx)
        blocks.append((r0, ws, t))
    c = np.arange(GRID_W)
    q_start = np.clip(c - kw // 2, 0, GRID_W - kw)
    col_ok = (c[None, :] >= q_start[:, None]) & (c[None, :] < q_start[:, None] + kw)
    dc_idx = np.clip(c[None, :] - c[:, None] + kw - 1, 0, 2 * kw - 2)
    return blocks, np.stack(types), col_ok, dc_idx


def _nbr_bias_tiles(rel_bias, plan):
    _, blk_idx, col_ok, dc_idx = plan
    heads, n_dr, _ = rel_bias.shape
    tb = jnp.where(col_ok[None, None], rel_bias[:, :, dc_idx], NEG_INF)
    tb = jnp.concatenate([tb, jnp.full((heads, 1, GRID_W, GRID_W), NEG_INF, F32)], axis=1)
    tiles = tb[:, blk_idx]
    nt = blk_idx.shape[0]
    tiles = tiles.transpose(0, 1, 2, 4, 3, 5).reshape(heads, nt, NA_QROWS * GRID_W, NA_KROWS * GRID_W)
    return tiles.astype(BF16)


def _nbr_attn_kernel(q_ref, k_ref, v_ref, ck_ref, cv_ref, gp_ref, bias_ref, qg_ref, kg_ref, o_ref,
                     qn_scr, kn_scr, *, blocks, hd):
    scale = hd ** -0.5
    qb, kb = NA_QROWS * GRID_W, NA_KROWS * GRID_W
    nt_dims = (((1,), (1,)), ((), ()))
    qn_scr[...] = _rms(q_ref[...].astype(F32), qg_ref[...]).astype(BF16)
    kn_scr[...] = _rms(k_ref[...].astype(F32), kg_ref[...]).astype(BF16)
    ck = ck_ref[0, 0, 0].astype(BF16)
    cv = cv_ref[0, 0, 0].astype(BF16)
    for r0, ws, t in blocks:
        qs = slice(r0 * GRID_W, r0 * GRID_W + qb)
        ks = slice(ws * GRID_W, ws * GRID_W + kb)
        q = qn_scr[qs, :]
        s_loc = lax.dot_general(q, kn_scr[ks, :], nt_dims, preferred_element_type=F32) * scale
        s_loc = s_loc + bias_ref[0, t].astype(F32)
        s_ctx = lax.dot_general(q, ck, nt_dims, preferred_element_type=F32) * scale
        m = jnp.maximum(jnp.max(s_loc, axis=-1, keepdims=True), jnp.max(s_ctx, axis=-1, keepdims=True))
        p_loc = jnp.exp(s_loc - m)
        p_ctx = jnp.exp(s_ctx - m)
        denom = jnp.sum(p_loc, axis=-1, keepdims=True) + jnp.sum(p_ctx, axis=-1, keepdims=True)
        o = jnp.dot(p_loc.astype(BF16), v_ref[ks, :], preferred_element_type=F32)
        o = (o + jnp.dot(p_ctx.astype(BF16), cv, preferred_element_type=F32)) / denom
        o_ref[qs, :] = (o * _silu(gp_ref[qs, :].astype(F32))).astype(o_ref.dtype)


def _nbr_attn(u, cache_k, cache_v, layer, rel_bias, qg, kg, nb, seq, d_attn, gp_col0):
    hd = qg.shape[-1]
    heads = d_attn // hd
    t = u.shape[0]
    past = cache_k.shape[3]
    rows = seq // GRID_W
    kh = min((rel_bias.shape[1] + 1) // 2, rows)
    kw = (rel_bias.shape[2] + 1) // 2
    plan = _nbr_plan(rows, kh, kw)
    bias = _nbr_bias_tiles(rel_bias, plan)
    gp_blk0 = gp_col0 // hd
    cache_spec = pl.BlockSpec((1, 1, 1, past, hd), lambda h, b: (b, layer, h, 0, 0))
    return pl.pallas_call(
        functools.partial(_nbr_attn_kernel, blocks=tuple(plan[0]), hd=hd),
        grid=(heads, nb),
        in_specs=[pl.BlockSpec((seq, hd), lambda h, b: (b, h)),
                  pl.BlockSpec((seq, hd), lambda h, b: (b, heads + h)),
                  pl.BlockSpec((seq, hd), lambda h, b: (b, 2 * heads + h)),
                  cache_spec, cache_spec,
                  pl.BlockSpec((seq, hd), lambda h, b: (b, gp_blk0 + h)),
                  pl.BlockSpec((1,) + bias.shape[1:], lambda h, b: (h, 0, 0, 0)),
                  pl.BlockSpec((1, hd), lambda h, b: (0, 0)),
                  pl.BlockSpec((1, hd), lambda h, b: (0, 0))],
        out_specs=pl.BlockSpec((seq, hd), lambda h, b: (b, h)),
        out_shape=jax.ShapeDtypeStruct((t, d_attn), BF16),
        scratch_shapes=[pltpu.VMEM((seq, hd), BF16), pltpu.VMEM((seq, hd), BF16)],
        compiler_params=_cparams("parallel", "parallel"),
    )(u, u, u, cache_k, cache_v, u, bias, qg.reshape(1, hd), kg.reshape(1, hd))


def _conv_kernel(a_ref, b_ref, ap_ref, bp_ref, an_ref, bn_ref, gp_ref, dw_ref, dwb_ref, lng_ref, lnb_ref,
                 wpw_ref, o_ref, h_scr, c_scr, z_scr, *, tiles_per_seq, width):
    tm, dc = a_ref.shape
    nc = dc // LANES
    pos = pl.program_id(0) % tiles_per_seq
    keep_prev = (pos > 0).astype(F32)
    keep_next = (pos < tiles_per_seq - 1).astype(F32)

    for c in range(nc):
        cols = slice(c * LANES, (c + 1) * LANES)
        h_scr[c, 0:CONV_HALO, :] = keep_prev * (
            ap_ref[:, cols].astype(F32) * jax.nn.sigmoid(bp_ref[:, cols].astype(F32)))
        h_scr[c, CONV_HALO:CONV_HALO + tm, :] = (
            a_ref[:, cols].astype(F32) * jax.nn.sigmoid(b_ref[:, cols].astype(F32)))
        h_scr[c, CONV_HALO + tm:, :] = keep_next * (
            an_ref[:, cols].astype(F32) * jax.nn.sigmoid(bn_ref[:, cols].astype(F32)))

    off = CONV_HALO - width // 2

    def chunk(c, carry):
        for rb in range(tm // CONV_ROWS):
            acc = jnp.zeros((CONV_ROWS, LANES), F32)
            for k in range(width):
                acc = acc + dw_ref[c, k:k + 1, :] * h_scr[c, pl.ds(rb * CONV_ROWS + off + k, CONV_ROWS), :]
            c_scr[c, rb * CONV_ROWS:(rb + 1) * CONV_ROWS, :] = acc + dwb_ref[c]
        return carry

    lax.fori_loop(0, nc, chunk, 0)

    s1 = jnp.zeros((tm, LANES), F32)
    for c in range(nc):
        s1 = s1 + c_scr[c]
    mu = jnp.sum(s1, axis=-1, keepdims=True) / dc
    s2 = jnp.zeros((tm, LANES), F32)
    for c in range(nc):
        xc = c_scr[c] - mu
        s2 = s2 + xc * xc
    rstd = lax.rsqrt(jnp.sum(s2, axis=-1, keepdims=True) / dc + EPS)
    for c in range(nc):
        cols = slice(c * LANES, (c + 1) * LANES)
        y = (c_scr[c] - mu) * rstd * lng_ref[:, cols] + lnb_ref[:, cols]
        z_scr[:, cols] = _silu(y).astype(BF16)
    out = jnp.dot(z_scr[...], wpw_ref[...], preferred_element_type=F32)
    o_ref[...] = (out * _silu(gp_ref[...].astype(F32))).astype(o_ref.dtype)


def _conv_module(u, dw, dw_b, ln_g, ln_b, w_pw, seq, in_col0, gp_col0):
    width, dc = dw.shape
    t = u.shape[0]
    tm = CONV_TILE
    assert seq % tm == 0 and width // 2 <= CONV_HALO and tm % CONV_HALO == 0 and dc % LANES == 0
    nc = dc // LANES
    tiles_per_seq = seq // tm
    hb = tm // CONV_HALO
    n_hblk = t // CONV_HALO
    a_blk, gp_blk = in_col0 // dc, gp_col0 // dc
    wpad = -width % 8
    dw3 = jnp.pad(dw, ((0, wpad), (0, 0))).reshape(width + wpad, nc, LANES).transpose(1, 0, 2)
    prev = lambda i: jnp.maximum(i * hb - 1, 0)
    nxt = lambda i: jnp.minimum((i + 1) * hb, n_hblk - 1)
    vec = lambda x: x.reshape(1, dc)
    return pl.pallas_call(
        functools.partial(_conv_kernel, tiles_per_seq=tiles_per_seq, width=width),
        grid=(t // tm,),
        in_specs=[pl.BlockSpec((tm, dc), lambda i: (i, a_blk)),
                  pl.BlockSpec((tm, dc), lambda i: (i, a_blk + 1)),
                  pl.BlockSpec((CONV_HALO, dc), lambda i: (prev(i), a_blk)),
                  pl.BlockSpec((CONV_HALO, dc), lambda i: (prev(i), a_blk + 1)),
                  pl.BlockSpec((CONV_HALO, dc), lambda i: (nxt(i), a_blk)),
                  pl.BlockSpec((CONV_HALO, dc), lambda i: (nxt(i), a_blk + 1)),
                  pl.BlockSpec((tm, dc), lambda i: (i, gp_blk)),
                  pl.BlockSpec((nc, width + wpad, LANES), lambda i: (0, 0, 0)),
                  pl.BlockSpec((nc, 1, LANES), lambda i: (0, 0, 0)),
                  pl.BlockSpec((1, dc), lambda i: (0, 0)),
                  pl.BlockSpec((1, dc), lambda i: (0, 0)),
                  pl.BlockSpec((dc, dc), lambda i: (0, 0))],
        out_specs=pl.BlockSpec((tm, dc), lambda i: (i, 0)),
        out_shape=jax.ShapeDtypeStruct((t, dc), BF16),
        scratch_shapes=[pltpu.VMEM((nc, tm + 2 * CONV_HALO, LANES), F32),
                        pltpu.VMEM((nc, tm, LANES), F32),
                        pltpu.VMEM((tm, dc), BF16)],
        compiler_params=_cparams("parallel"),
    )(u, u, u, u, u, u, u, dw3, dw_b.reshape(nc, 1, LANES), vec(ln_g), vec(ln_b), w_pw)


def _trunk(x, nb, seq, row_of, mod, caches, wts):
    d = x.shape[1]
    depth = mod.shape[0]
    new_k, new_v = [], []
    for i in range(depth):
        mod3 = mod[i][:, None, :]
        h = _norm_mod(x, wts["norm_g"][i], mod3, row_of)
        j = i // 2
        if i % 2 == 0:
            u = _matmul(h, wts["w_in_even"][j])
            d_f = wts["w_fourier"].shape[1] * wts["w_fourier"].shape[2]
            d_inner = d_f + wts["w_pool"].shape[1] * wts["w_pool"].shape[2]
            y1 = _fourier(u, wts["w_fourier"][j], nb, seq, gp_col0=d_inner)
            y2 = _pool(u, wts["w_pool"][j], wts["pool_scale"][j], nb, seq, in_col0=d_f,
                       gp_col0=d_inner + d_f)
            x = _out_proj(y1, y2, wts["w_out_even"][j], x, mod3, row_of)
        else:
            u = _matmul(h, wts["w_in_odd"][j])
            d_conv = wts["conv_dw"].shape[2]
            d_attn = (u.shape[1] - 3 * d_conv) // 4
            gp_col0 = 3 * d_attn + 2 * d_conv
            if caches is None:
                y1, sk, sv = _ctx_attn(u, wts["q_norm_g"][j], wts["k_norm_g"][j], nb, seq, d_attn, gp_col0)
                new_k.append(sk)
                new_v.append(sv)
            else:
                y1 = _nbr_attn(u, caches[0], caches[1], j, wts["rel_bias"][j], wts["q_norm_g"][j],
                               wts["k_norm_g"][j], nb, seq, d_attn, gp_col0)
            y2 = _conv_module(u, wts["conv_dw"][j], wts["conv_dw_b"][j], wts["conv_ln_g"][j],
                              wts["conv_ln_b"][j], wts["w_conv_pw"][j], seq, in_col0=3 * d_attn,
                              gp_col0=gp_col0 + d_attn)
            x = _out_proj(y1, y2, wts["w_out_odd"][j], x, mod3, row_of)
    return x, new_k, new_v


def kernel(x_prompt, x_sample, cache_k, cache_v, c, c_ctx, norm_g, w_ada, b_ada, w_in_even, w_out_even,
           w_fourier, w_pool, pool_scale, w_in_odd, w_out_odd, q_norm_g, k_norm_g, rel_bias, conv_dw,
           conv_dw_b, conv_ln_g, conv_ln_b, w_conv_pw):
    nb_p, seq_p, d = x_prompt.shape
    nb_s, seq_s, _ = x_sample.shape
    n_rows = -(-(1 + nb_s) // 8) * 8
    cond = jnp.concatenate([c_ctx[None, :], c, jnp.zeros((n_rows - 1 - nb_s, d), F32)], axis=0)
    mod = _adaln(cond, w_ada, b_ada)
    wts = dict(norm_g=norm_g, pool_scale=pool_scale, q_norm_g=q_norm_g, k_norm_g=k_norm_g,
               rel_bias=rel_bias, conv_dw=conv_dw, conv_dw_b=conv_dw_b, conv_ln_g=conv_ln_g,
               conv_ln_b=conv_ln_b)
    for name, w in (("w_in_even", w_in_even), ("w_out_even", w_out_even), ("w_fourier", w_fourier),
                    ("w_pool", w_pool), ("w_in_odd", w_in_odd), ("w_out_odd", w_out_odd),
                    ("w_conv_pw", w_conv_pw)):
        wts[name] = w.astype(BF16)
    y_p, ks, vs = _trunk(x_prompt.reshape(nb_p * seq_p, d), nb_p, seq_p, lambda r: 0, mod, None, wts)
    y_s, _, _ = _trunk(x_sample.reshape(nb_s * seq_s, d), nb_s, seq_s, lambda r: 1 + r // seq_s, mod,
                       (cache_k, cache_v), wts)
    state_k = jnp.concatenate(ks, axis=1)
    state_v = jnp.concatenate(vs, axis=1)
    return (y_p.reshape(nb_p, seq_p, d), y_s.reshape(nb_s, seq_s, d), state_k, state_v)
```

```python
import functools

import numpy as np
import jax
import jax.numpy as jnp
from jax import lax
from jax.experimental import pallas as pl
from jax.experimental.pallas import tpu as pltpu

F32 = jnp.float32
BF16 = jnp.bfloat16

GRID_W = 64
POOL_WINDOWS = (2, 4, 8, 16)
EPS = 1e-6
NEG_INF = -1e30
LOG2E = float(np.log2(np.e))

LANES = 128
V7X_VMEM_LIMIT_BYTES = 56 * 1024 * 1024

NA_QROWS = 8
NA_KROWS = 16
POOL_TILE = 256
POOL_HALO = 128
POOL_BLOCK_ROWS = 2048
CONV_TILE = 256
CONV_HALO = 16
CONV_ROWS = 64


def _cparams(*sem):
    return pltpu.CompilerParams(dimension_semantics=sem, vmem_limit_bytes=V7X_VMEM_LIMIT_BYTES)


def _pick(n, pref, unit=LANES):
    if n <= pref:
        return n
    t = (pref // unit) * unit
    while t > unit and n % t:
        t -= unit
    assert n % t == 0, (n, pref, unit)
    return t


def _silu(x):
    return x * jax.nn.sigmoid(x)


def _rms(x, g):
    return x * lax.rsqrt(jnp.mean(x * x, axis=-1, keepdims=True) + EPS) * g


def _adaln_kernel(cond_ref, w_ref, b_ref, o_ref):
    s = _silu(cond_ref[...]).astype(BF16)
    o_ref[0] = jnp.dot(s, w_ref[0].astype(BF16), preferred_element_type=F32) + b_ref[0]


def _adaln(cond, w_ada, b_ada):
    depth, d, n = w_ada.shape
    r = cond.shape[0]
    tn = _pick(n, 1024)
    return pl.pallas_call(
        _adaln_kernel,
        grid=(depth, n // tn),
        in_specs=[pl.BlockSpec((r, d), lambda l, j: (0, 0)),
                  pl.BlockSpec((1, d, tn), lambda l, j: (l, 0, j)),
                  pl.BlockSpec((1, 1, tn), lambda l, j: (l, 0, j))],
        out_specs=pl.BlockSpec((1, r, tn), lambda l, j: (l, 0, j)),
        out_shape=jax.ShapeDtypeStruct((depth, r, n), F32),
        compiler_params=_cparams("parallel", "parallel"),
    )(cond, w_ada, b_ada.reshape(depth, 1, n))


def _norm_mod_kernel(x_ref, g_ref, shift_ref, scale_ref, o_ref):
    y = _rms(x_ref[...], g_ref[...])
    o_ref[...] = (y * (1.0 + scale_ref[0]) + shift_ref[0]).astype(o_ref.dtype)


def _norm_mod(x, g, mod3, row_of):
    t, d = x.shape
    tm = _pick(t, 512, 8)
    return pl.pallas_call(
        _norm_mod_kernel,
        grid=(t // tm,),
        in_specs=[pl.BlockSpec((tm, d), lambda i: (i, 0)),
                  pl.BlockSpec((1, d), lambda i: (0, 0)),
                  pl.BlockSpec((1, 1, d), lambda i: (row_of(i * tm), 0, 0)),
                  pl.BlockSpec((1, 1, d), lambda i: (row_of(i * tm), 0, 1))],
        out_specs=pl.BlockSpec((tm, d), lambda i: (i, 0)),
        out_shape=jax.ShapeDtypeStruct((t, d), BF16),
        compiler_params=_cparams("parallel"),
    )(x, g.reshape(1, d), mod3, mod3)


def _matmul_kernel(a_ref, w_ref, o_ref):
    o_ref[...] = jnp.dot(a_ref[...], w_ref[...], preferred_element_type=F32).astype(o_ref.dtype)


def _matmul(a, w, out_dtype=BF16):
    m, k = a.shape
    n = w.shape[1]
    bm, bn = _pick(m, 1024, 8), _pick(n, 1024)
    return pl.pallas_call(
        _matmul_kernel,
        grid=(m // bm, n // bn),
        in_specs=[pl.BlockSpec((bm, k), lambda i, j: (i, 0)),
                  pl.BlockSpec((k, bn), lambda i, j: (0, j))],
        out_specs=pl.BlockSpec((bm, bn), lambda i, j: (i, j)),
        out_shape=jax.ShapeDtypeStruct((m, n), out_dtype),
        compiler_params=_cparams("parallel", "parallel"),
    )(a, w)


def _out_proj_kernel(a1_ref, a2_ref, w1_ref, w2_ref, x_ref, gate_ref, o_ref):
    acc = jnp.dot(a1_ref[...], w1_ref[...], preferred_element_type=F32)
    acc = acc + jnp.dot(a2_ref[...], w2_ref[...], preferred_element_type=F32)
    o_ref[...] = x_ref[...] + gate_ref[0] * acc


def _out_proj(a1, a2, w, x, mod3, row_of):
    t, k1 = a1.shape
    d = w.shape[1]
    assert a2.shape == (t, k1) and w.shape[0] == 2 * k1
    bm, bn = _pick(t, 1024, 8), _pick(d, 512)
    gate_blk = 2 * (d // bn)
    return pl.pallas_call(
        _out_proj_kernel,
        grid=(t // bm, d // bn),
        in_specs=[pl.BlockSpec((bm, k1), lambda i, j: (i, 0)),
                  pl.BlockSpec((bm, k1), lambda i, j: (i, 0)),
                  pl.BlockSpec((k1, bn), lambda i, j: (0, j)),
                  pl.BlockSpec((k1, bn), lambda i, j: (1, j)),
                  pl.BlockSpec((bm, bn), lambda i, j: (i, j)),
                  pl.BlockSpec((1, 1, bn), lambda i, j: (row_of(i * bm), 0, gate_blk + j))],
        out_specs=pl.BlockSpec((bm, bn), lambda i, j: (i, j)),
        out_shape=jax.ShapeDtypeStruct((t, d), F32),
        compiler_params=_cparams("parallel", "parallel"),
    )(a1, a2, w, w, x, mod3)


def _dft_cos_sin(n):
    idx = np.arange(n, dtype=np.int64)
    ang = 2.0 * np.pi * ((idx[:, None] * idx[None, :]) % n) / n
    return np.cos(ang) / np.sqrt(n), np.sin(ang) / np.sqrt(n)


def _fourier_kernel(u_ref, gp_ref, csc_ref, csl_ref, wf_ref, o_ref, p_scr, *, seq, cg):
    @pl.when(pl.program_id(2) == 0)
    def _():
        p = jnp.dot(u_ref[...], csc_ref[...], preferred_element_type=F32)
        p_scr[0:seq, :] = p[:, :cg].astype(BF16)
        p_scr[seq:2 * seq, :] = p[:, cg:].astype(BF16)

    f = jnp.dot(csl_ref[...], p_scr[...], preferred_element_type=F32)
    y = jnp.dot(f.astype(BF16), wf_ref[0], preferred_element_type=F32)
    o_ref[...] = (y * _silu(gp_ref[...].astype(F32))).astype(o_ref.dtype)


def _fourier(u, wf, nb, seq, gp_col0):
    groups, cg, _ = wf.shape
    t = u.shape[0]
    tr = _pick(seq, 512, 8)
    rt = seq // tr
    cc, sc = _dft_cos_sin(cg)
    cl, sl = _dft_cos_sin(seq)
    csc = jnp.asarray(np.concatenate([cc, sc], axis=1), BF16)
    csl = jnp.asarray(np.concatenate([cl, -sl], axis=1), BF16)
    gp_blk0 = gp_col0 // cg
    return pl.pallas_call(
        functools.partial(_fourier_kernel, seq=seq, cg=cg),
        grid=(nb, groups, rt),
        in_specs=[pl.BlockSpec((seq, cg), lambda b, g, r: (b, g)),
                  pl.BlockSpec((tr, cg), lambda b, g, r: (b * rt + r, gp_blk0 + g)),
                  pl.BlockSpec((cg, 2 * cg), lambda b, g, r: (0, 0)),
                  pl.BlockSpec((tr, 2 * seq), lambda b, g, r: (r, 0)),
                  pl.BlockSpec((1, cg, cg), lambda b, g, r: (g, 0, 0))],
        out_specs=pl.BlockSpec((tr, cg), lambda b, g, r: (b * rt + r, g)),
        out_shape=jax.ShapeDtypeStruct((t, groups * cg), BF16),
        scratch_shapes=[pltpu.VMEM((2 * seq, cg), BF16)],
        compiler_params=_cparams("parallel", "parallel", "arbitrary"),
    )(u, u, csc, csl, wf)


def _pool_tables(seq):
    r = np.arange(POOL_TILE)[:, None]
    s = np.arange(POOL_TILE + 2 * POOL_HALO)[None, :] - POOL_HALO
    tpos = np.arange(seq)
    bands, cnts = [], []
    for win in POOL_WINDOWS:
        half = win // 2
        assert half <= POOL_HALO
        bands.append(((s >= r - half) & (s <= r + half - 1)).astype(np.float32))
        cnts.append(np.minimum(tpos + half, seq) - np.maximum(tpos - half, 0))
    return np.stack(bands), np.stack(cnts).astype(np.float32)[:, :, None]


def _pool_kernel(u_ref, gp_ref, band_ref, cnt_ref, w_ref, ps_ref, o_ref, *, seq):
    tr = POOL_TILE
    for s0 in range(0, u_ref.shape[0], seq):
        for t in range(seq // tr):
            base = tr * t - POOL_HALO
            lo, hi = max(base, 0), min(base + tr + 2 * POOL_HALO, seq)
            rows = slice(s0 + tr * t, s0 + tr * (t + 1))
            wsum = jnp.dot(band_ref[0, :, lo - base:hi - base], u_ref[s0 + lo:s0 + hi, :],
                           preferred_element_type=F32)
            cnt = cnt_ref[0, tr * t:tr * (t + 1), :]
            d = (wsum / cnt - u_ref[rows, :].astype(F32)).astype(BF16)
            y = jnp.dot(d, w_ref[0], preferred_element_type=F32) * ps_ref[...]
            o_ref[rows, :] = (y * _silu(gp_ref[rows, :].astype(F32))).astype(o_ref.dtype)


def _pool(u, wp, pool_scale, nb, seq, in_col0, gp_col0):
    groups, cg, _ = wp.shape
    assert groups == len(POOL_WINDOWS) and seq % POOL_TILE == 0
    t = u.shape[0]
    band, cnt = _pool_tables(seq)
    in_blk0, gp_blk0 = in_col0 // cg, gp_col0 // cg
    spb = max(1, min(nb, POOL_BLOCK_ROWS // seq))
    while nb % spb:
        spb -= 1
    rows = spb * seq
    return pl.pallas_call(
        functools.partial(_pool_kernel, seq=seq),
        grid=(nb // spb, groups),
        in_specs=[pl.BlockSpec((rows, cg), lambda b, g: (b, in_blk0 + g)),
                  pl.BlockSpec((rows, cg), lambda b, g: (b, gp_blk0 + g)),
                  pl.BlockSpec((1,) + band.shape[1:], lambda b, g: (g, 0, 0)),
                  pl.BlockSpec((1, seq, 1), lambda b, g: (g, 0, 0)),
                  pl.BlockSpec((1, cg, cg), lambda b, g: (g, 0, 0)),
                  pl.BlockSpec((1, cg), lambda b, g: (0, g))],
        out_specs=pl.BlockSpec((rows, cg), lambda b, g: (b, g)),
        out_shape=jax.ShapeDtypeStruct((t, groups * cg), BF16),
        compiler_params=_cparams("parallel", "parallel"),
    )(u, u, jnp.asarray(band, BF16), jnp.asarray(cnt, F32), wp, pool_scale.reshape(1, groups * cg))


def _ctx_attn_kernel(q_ref, k_ref, v_ref, gp_ref, qg_ref, kg_ref, o_ref, sk_ref, sv_ref, *, heads, hd):
    qscale = hd ** -0.5 * LOG2E
    for h in range(heads):
        cols = slice(h * hd, (h + 1) * hd)
        qn = _rms(q_ref[:, cols].astype(F32), qg_ref[...]) * qscale
        kn = _rms(k_ref[:, cols].astype(F32), kg_ref[...])
        v = v_ref[:, cols]
        sk_ref[0, 0, h] = kn
        sv_ref[0, 0, h] = v.astype(F32)
        s = lax.dot_general(qn.astype(BF16), kn.astype(BF16), (((1,), (1,)), ((), ())),
                            preferred_element_type=F32)
        p = jnp.exp2(s - jnp.max(s, axis=-1, keepdims=True))
        o = jnp.dot(p.astype(BF16), v, preferred_element_type=F32) / jnp.sum(p, axis=-1, keepdims=True)
        o_ref[:, cols] = (o * _silu(gp_ref[:, cols].astype(F32))).astype(o_ref.dtype)


def _ctx_attn(u, qg, kg, nb, seq, d_attn, gp_col0):
    hd = qg.shape[-1]
    heads = d_attn // hd
    t = u.shape[0]
    state = jax.ShapeDtypeStruct((nb, 1, heads, seq, hd), F32)
    state_spec = pl.BlockSpec((1, 1, heads, seq, hd), lambda b: (b, 0, 0, 0, 0))
    gp_blk = gp_col0 // d_attn
    return pl.pallas_call(
        functools.partial(_ctx_attn_kernel, heads=heads, hd=hd),
        grid=(nb,),
        in_specs=[pl.BlockSpec((seq, d_attn), lambda b: (b, 0)),
                  pl.BlockSpec((seq, d_attn), lambda b: (b, 1)),
                  pl.BlockSpec((seq, d_attn), lambda b: (b, 2)),
                  pl.BlockSpec((seq, d_attn), lambda b: (b, gp_blk)),
                  pl.BlockSpec((1, hd), lambda b: (0, 0)),
                  pl.BlockSpec((1, hd), lambda b: (0, 0))],
        out_specs=[pl.BlockSpec((seq, d_attn), lambda b: (b, 0)), state_spec, state_spec],
        out_shape=[jax.ShapeDtypeStruct((t, d_attn), BF16), state, state],
        compiler_params=_cparams("parallel"),
    )(u, u, u, u, qg.reshape(1, hd), kg.reshape(1, hd))


def _nbr_plan(rows, kh, n_kh):
    assert rows % NA_QROWS == 0 and rows >= NA_KROWS
    n_dr = 2 * n_kh - 1
    blocks, types = [], []
    for r0 in range(0, rows, NA_QROWS):
        ws = int(np.clip(r0 - kh // 2, 0, rows - NA_KROWS))
        idx = np.full((NA_QROWS, NA_KROWS), n_dr, np.int32)
        for i in range(NA_QROWS):
            r = r0 + i
            start = int(np.clip(r - kh // 2, 0, rows - kh))
            assert ws <= start and start + kh <= ws + NA_KROWS
            for j in range(NA_KROWS):
                rk = ws + j
                if start <= rk < start + kh:
                    idx[i, j] = rk - r + n_kh - 1
        for t, known in enumerate(types):
            if np.array_equal(known, idx):
                break
        else:
            t = len(types)
            types.append(idx)
        blocks.append((r0, ws, t))
    return tuple(blocks), np.stack(types)


def _nbr_build_bias(rb_ref, bias_scr, head, blk_idx, n_dr, n_dc, kw):
    w = GRID_W
    c = lax.broadcasted_iota(jnp.int32, (w, 2 * w), 0)
    lane = lax.broadcasted_iota(jnp.int32, (w, 2 * w), 1)
    cp = lane % w
    q_start = jnp.clip(c - kw // 2, 0, w - kw)
    diff = jnp.where(cp >= q_start, jnp.where(cp < q_start + kw, cp - c + (kw - 1), -1), -1)
    neg = jnp.full((w, 2 * w), NEG_INF, F32)
    pairs = []
    for dr in range(n_dr):
        blk = neg
        for d in range(n_dc):
            blk = jnp.where(diff == d, rb_ref[(head * n_dr + dr) * n_dc + d] * LOG2E, blk)
        pairs.append(blk)
    pairs.append(neg)
    low = lane < w
    nt, qr, kr = blk_idx.shape
    for t in range(nt):
        for i in range(qr):
            for j in range(0, kr, 2):
                e, o = int(blk_idx[t, i, j]), int(blk_idx[t, i, j + 1])
                blk = pairs[e] if e == o else jnp.where(low, pairs[e], pairs[o])
                bias_scr[t, i * w:(i + 1) * w, j * w:(j + 2) * w] = blk


def _nbr_attn_kernel(rb_ref, q_ref, k_ref, v_ref, ck_ref, cv_ref, gp_ref, qg_ref, kg_ref, o_ref,
                     qn_scr, kn_scr, bias_scr, s_all, p_all, den_all, *, blocks, blk_idx, hd, kw, n_dr, n_dc):
    @pl.when(pl.program_id(1) == 0)
    def _():
        _nbr_build_bias(rb_ref, bias_scr, pl.program_id(0), blk_idx, n_dr, n_dc, kw)

    qscale = hd ** -0.5 * LOG2E
    qb, kb = NA_QROWS * GRID_W, NA_KROWS * GRID_W
    nt_dims = (((1,), (1,)), ((), ()))
    qn_scr[...] = (_rms(q_ref[...].astype(F32), qg_ref[...]) * qscale).astype(BF16)
    kn_scr[...] = _rms(k_ref[...].astype(F32), kg_ref[...]).astype(BF16)
    ck = ck_ref[0, 0, 0].astype(BF16)
    cv = cv_ref[0, 0, 0].astype(BF16)
    past = ck.shape[0]
    n_chunks = (kb + past) // LANES
    for n, (r0, ws, t) in enumerate(blocks):
        s_scr, p_scr, den_scr = s_all.at[n % 2], p_all.at[n % 2], den_all.at[n % 2]
        qs = slice(r0 * GRID_W, r0 * GRID_W + qb)
        ks = slice(ws * GRID_W, ws * GRID_W + kb)
        q = qn_scr[qs, :]
        s_scr[:, 0:kb] = lax.dot_general(q, kn_scr[ks, :], nt_dims, preferred_element_type=F32) + bias_scr[t]
        s_scr[:, kb:] = lax.dot_general(q, ck, nt_dims, preferred_element_type=F32)
        for i in range(NA_QROWS):
            rows = slice(i * GRID_W, (i + 1) * GRID_W)
            live = [cc for cc in range(n_chunks)
                    if cc * LANES >= kb or (blk_idx[t, i, 2 * cc:2 * cc + 2] < n_dr).any()]
            m = s_scr[rows, live[0] * LANES:(live[0] + 1) * LANES]
            for cc in live[1:]:
                m = jnp.maximum(m, s_scr[rows, cc * LANES:(cc + 1) * LANES])
            m = jnp.max(m, axis=-1, keepdims=True)
            acc = jnp.zeros((GRID_W, LANES), F32)
            for cc in range(n_chunks):
                cols = slice(cc * LANES, (cc + 1) * LANES)
                if cc in live:
                    p = jnp.exp2(s_scr[rows, cols] - m)
                    acc = acc + p
                    p_scr[rows, cols] = p.astype(BF16)
                else:
                    p_scr[rows, cols] = jnp.zeros((GRID_W, LANES), BF16)
            den_scr[rows, :] = jnp.sum(acc, axis=-1, keepdims=True)
        o = jnp.dot(p_scr[:, 0:kb], v_ref[ks, :], preferred_element_type=F32)
        o = (o + jnp.dot(p_scr[:, kb:], cv, preferred_element_type=F32)) / den_scr[...]
        o_ref[qs, :] = (o * _silu(gp_ref[qs, :].astype(F32))).astype(o_ref.dtype)


def _nbr_attn(u, cache_k, cache_v, layer, rel_bias, qg, kg, nb, seq, d_attn, gp_col0):
    hd = qg.shape[-1]
    heads = d_attn // hd
    t = u.shape[0]
    past = cache_k.shape[3]
    rows = seq // GRID_W
    _, n_dr, n_dc = rel_bias.shape
    n_kh, kw = (n_dr + 1) // 2, (n_dc + 1) // 2
    assert 2 * GRID_W == LANES and kw <= GRID_W
    blocks, blk_idx = _nbr_plan(rows, min(n_kh, rows), n_kh)
    gp_blk0 = gp_col0 // hd
    cache_spec = pl.BlockSpec((1, 1, 1, past, hd), lambda h, b: (b, layer, h, 0, 0))
    return pl.pallas_call(
        functools.partial(_nbr_attn_kernel, blocks=blocks, blk_idx=blk_idx, hd=hd, kw=kw, n_dr=n_dr,
                          n_dc=n_dc),
        grid=(heads, nb),
        in_specs=[pl.BlockSpec(memory_space=pltpu.SMEM),
                  pl.BlockSpec((seq, hd), lambda h, b: (b, h)),
                  pl.BlockSpec((seq, hd), lambda h, b: (b, heads + h)),
                  pl.BlockSpec((seq, hd), lambda h, b: (b, 2 * heads + h)),
                  cache_spec, cache_spec,
                  pl.BlockSpec((seq, hd), lambda h, b: (b, gp_blk0 + h)),
                  pl.BlockSpec((1, hd), lambda h, b: (0, 0)),
                  pl.BlockSpec((1, hd), lambda h, b: (0, 0))],
        out_specs=pl.BlockSpec((seq, hd), lambda h, b: (b, h)),
        out_shape=jax.ShapeDtypeStruct((t, d_attn), BF16),
        scratch_shapes=[pltpu.VMEM((seq, hd), BF16), pltpu.VMEM((seq, hd), BF16),
                        pltpu.VMEM((blk_idx.shape[0], NA_QROWS * GRID_W, NA_KROWS * GRID_W), F32),
                        pltpu.VMEM((2, NA_QROWS * GRID_W, NA_KROWS * GRID_W + past), F32),
                        pltpu.VMEM((2, NA_QROWS * GRID_W, NA_KROWS * GRID_W + past), BF16),
                        pltpu.VMEM((2, NA_QROWS * GRID_W, 1), F32)],
        compiler_params=_cparams("arbitrary", "arbitrary"),
    )(rel_bias.reshape(-1), u, u, u, cache_k, cache_v, u, qg.reshape(1, hd), kg.reshape(1, hd))


def _conv_kernel(a_ref, b_ref, ap_ref, bp_ref, an_ref, bn_ref, gp_ref, dw_ref, dwb_ref, lng_ref, lnb_ref,
                 wpw_ref, o_ref, h_scr, c_scr, z_scr, *, tiles_per_seq, width):
    tm, dc = a_ref.shape
    nc = dc // LANES
    pos = pl.program_id(0) % tiles_per_seq
    keep_prev = (pos > 0).astype(F32)
    keep_next = (pos < tiles_per_seq - 1).astype(F32)

    for c in range(nc):
        cols = slice(c * LANES, (c + 1) * LANES)
        h_scr[c, 0:CONV_HALO, :] = keep_prev * (
            ap_ref[:, cols].astype(F32) * jax.nn.sigmoid(bp_ref[:, cols].astype(F32)))
        h_scr[c, CONV_HALO:CONV_HALO + tm, :] = (
            a_ref[:, cols].astype(F32) * jax.nn.sigmoid(b_ref[:, cols].astype(F32)))
        h_scr[c, CONV_HALO + tm:, :] = keep_next * (
            an_ref[:, cols].astype(F32) * jax.nn.sigmoid(bn_ref[:, cols].astype(F32)))

    off = CONV_HALO - width // 2

    def chunk(c, carry):
        for rb in range(tm // CONV_ROWS):
            acc = jnp.zeros((CONV_ROWS, LANES), F32)
            for k in range(width):
                acc = acc + dw_ref[c, k:k + 1, :] * h_scr[c, pl.ds(rb * CONV_ROWS + off + k, CONV_ROWS), :]
            c_scr[c, rb * CONV_ROWS:(rb + 1) * CONV_ROWS, :] = acc + dwb_ref[c]
        return carry

    lax.fori_loop(0, nc, chunk, 0)

    s1 = jnp.zeros((tm, LANES), F32)
    for c in range(nc):
        s1 = s1 + c_scr[c]
    mu = jnp.sum(s1, axis=-1, keepdims=True) / dc
    s2 = jnp.zeros((tm, LANES), F32)
    for c in range(nc):
        xc = c_scr[c] - mu
        s2 = s2 + xc * xc
    rstd = lax.rsqrt(jnp.sum(s2, axis=-1, keepdims=True) / dc + EPS)
    for c in range(nc):
        cols = slice(c * LANES, (c + 1) * LANES)
        y = (c_scr[c] - mu) * rstd * lng_ref[:, cols] + lnb_ref[:, cols]
        z_scr[:, cols] = _silu(y).astype(BF16)
    out = jnp.dot(z_scr[...], wpw_ref[...], preferred_element_type=F32)
    o_ref[...] = (out * _silu(gp_ref[...].astype(F32))).astype(o_ref.dtype)


def _conv_module(u, dw, dw_b, ln_g, ln_b, w_pw, seq, in_col0, gp_col0):
    width, dc = dw.shape
    t = u.shape[0]
    tm = CONV_TILE
    assert seq % tm == 0 and width // 2 <= CONV_HALO and tm % CONV_HALO == 0 and dc % LANES == 0
    nc = dc // LANES
    tiles_per_seq = seq // tm
    hb = tm // CONV_HALO
    n_hblk = t // CONV_HALO
    a_blk, gp_blk = in_col0 // dc, gp_col0 // dc
    wpad = -width % 8
    dw3 = jnp.pad(dw, ((0, wpad), (0, 0))).reshape(width + wpad, nc, LANES).transpose(1, 0, 2)
    prev = lambda i: jnp.maximum(i * hb - 1, 0)
    nxt = lambda i: jnp.minimum((i + 1) * hb, n_hblk - 1)
    vec = lambda x: x.reshape(1, dc)
    return pl.pallas_call(
        functools.partial(_conv_kernel, tiles_per_seq=tiles_per_seq, width=width),
        grid=(t // tm,),
        in_specs=[pl.BlockSpec((tm, dc), lambda i: (i, a_blk)),
                  pl.BlockSpec((tm, dc), lambda i: (i, a_blk + 1)),
                  pl.BlockSpec((CONV_HALO, dc), lambda i: (prev(i), a_blk)),
                  pl.BlockSpec((CONV_HALO, dc), lambda i: (prev(i), a_blk + 1)),
                  pl.BlockSpec((CONV_HALO, dc), lambda i: (nxt(i), a_blk)),
                  pl.BlockSpec((CONV_HALO, dc), lambda i: (nxt(i), a_blk + 1)),
                  pl.BlockSpec((tm, dc), lambda i: (i, gp_blk)),
                  pl.BlockSpec((nc, width + wpad, LANES), lambda i: (0, 0, 0)),
                  pl.BlockSpec((nc, 1, LANES), lambda i: (0, 0, 0)),
                  pl.BlockSpec((1, dc), lambda i: (0, 0)),
                  pl.BlockSpec((1, dc), lambda i: (0, 0)),
                  pl.BlockSpec((dc, dc), lambda i: (0, 0))],
        out_specs=pl.BlockSpec((tm, dc), lambda i: (i, 0)),
        out_shape=jax.ShapeDtypeStruct((t, dc), BF16),
        scratch_shapes=[pltpu.VMEM((nc, tm + 2 * CONV_HALO, LANES), F32),
                        pltpu.VMEM((nc, tm, LANES), F32),
                        pltpu.VMEM((tm, dc), BF16)],
        compiler_params=_cparams("parallel"),
    )(u, u, u, u, u, u, u, dw3, dw_b.reshape(nc, 1, LANES), vec(ln_g), vec(ln_b), w_pw)


def _trunk(x, nb, seq, row_of, mod, caches, wts):
    d = x.shape[1]
    depth = mod.shape[0]
    new_k, new_v = [], []
    for i in range(depth):
        mod3 = mod[i][:, None, :]
        h = _norm_mod(x, wts["norm_g"][i], mod3, row_of)
        j = i // 2
        if i % 2 == 0:
            u = _matmul(h, wts["w_in_even"][j])
            d_f = wts["w_fourier"].shape[1] * wts["w_fourier"].shape[2]
            d_inner = d_f + wts["w_pool"].shape[1] * wts["w_pool"].shape[2]
            y1 = _fourier(u, wts["w_fourier"][j], nb, seq, gp_col0=d_inner)
            y2 = _pool(u, wts["w_pool"][j], wts["pool_scale"][j], nb, seq, in_col0=d_f,
                       gp_col0=d_inner + d_f)
            x = _out_proj(y1, y2, wts["w_out_even"][j], x, mod3, row_of)
        else:
            u = _matmul(h, wts["w_in_odd"][j])
            d_conv = wts["conv_dw"].shape[2]
            d_attn = (u.shape[1] - 3 * d_conv) // 4
            gp_col0 = 3 * d_attn + 2 * d_conv
            if caches is None:
                y1, sk, sv = _ctx_attn(u, wts["q_norm_g"][j], wts["k_norm_g"][j], nb, seq, d_attn, gp_col0)
                new_k.append(sk)
                new_v.append(sv)
            else:
                y1 = _nbr_attn(u, caches[0], caches[1], j, wts["rel_bias"][j], wts["q_norm_g"][j],
                               wts["k_norm_g"][j], nb, seq, d_attn, gp_col0)
            y2 = _conv_module(u, wts["conv_dw"][j], wts["conv_dw_b"][j], wts["conv_ln_g"][j],
                              wts["conv_ln_b"][j], wts["w_conv_pw"][j], seq, in_col0=3 * d_attn,
                              gp_col0=gp_col0 + d_attn)
            x = _out_proj(y1, y2, wts["w_out_odd"][j], x, mod3, row_of)
    return x, new_k, new_v


def kernel(x_prompt, x_sample, cache_k, cache_v, c, c_ctx, norm_g, w_ada, b_ada, w_in_even, w_out_even,
           w_fourier, w_pool, pool_scale, w_in_odd, w_out_odd, q_norm_g, k_norm_g, rel_bias, conv_dw,
           conv_dw_b, conv_ln_g, conv_ln_b, w_conv_pw):
    nb_p, seq_p, d = x_prompt.shape
    nb_s, seq_s, _ = x_sample.shape
    n_rows = -(-(1 + nb_s) // 8) * 8
    cond = jnp.concatenate([c_ctx[None, :], c, jnp.zeros((n_rows - 1 - nb_s, d), F32)], axis=0)
    mod = _adaln(cond, w_ada, b_ada)
    wts = dict(norm_g=norm_g, pool_scale=pool_scale, q_norm_g=q_norm_g, k_norm_g=k_norm_g,
               rel_bias=rel_bias, conv_dw=conv_dw, conv_dw_b=conv_dw_b, conv_ln_g=conv_ln_g,
               conv_ln_b=conv_ln_b)
    for name, w in (("w_in_even", w_in_even), ("w_out_even", w_out_even), ("w_fourier", w_fourier),
                    ("w_pool", w_pool), ("w_in_odd", w_in_odd), ("w_out_odd", w_out_odd),
                    ("w_conv_pw", w_conv_pw)):
        wts[name] = w.astype(BF16)
    y_p, ks, vs = _trunk(x_prompt.reshape(nb_p * seq_p, d), nb_p, seq_p, lambda r: 0, mod, None, wts)
    y_s, _, _ = _trunk(x_sample.reshape(nb_s * seq_s, d), nb_s, seq_s, lambda r: 1 + r // seq_s, mod,
                       (cache_k, cache_v), wts)
    state_k = jnp.concatenate(ks, axis=1)
    state_v = jnp.concatenate(vs, axis=1)
    return (y_p.reshape(nb_p, seq_p, d), y_s.reshape(nb_s, seq_s, d), state_k, state_v)
```

```python
import functools

import numpy as np
import jax
import jax.numpy as jnp
from jax import lax
from jax.experimental import pallas as pl
from jax.experimental.pallas import tpu as pltpu

F32 = jnp.float32
BF16 = jnp.bfloat16

GRID_W = 64
POOL_WINDOWS = (2, 4, 8, 16)
EPS = 1e-6
NEG_INF = -1e30
LOG2E = float(np.log2(np.e))

LANES = 128
V7X_VMEM_LIMIT_BYTES = 56 * 1024 * 1024
V7X_MXU_COLS = 256

NA_QROWS = 8
NA_KROWS = 16
POOL_TILE = 256
POOL_HALO = 128
POOL_BLOCK_ROWS = 2048
FOURIER_BLOCK_ROWS = 1024
CONV_TILE = 256
CONV_HALO = 16
CONV_ROWS = 64


def _cparams(*sem):
    return pltpu.CompilerParams(dimension_semantics=sem, vmem_limit_bytes=V7X_VMEM_LIMIT_BYTES)


def _pick(n, pref, unit=LANES):
    if n <= pref:
        return n
    t = (pref // unit) * unit
    while t > unit and n % t:
        t -= unit
    assert n % t == 0, (n, pref, unit)
    return t


def _sigmoid(x):
    return 0.5 * jnp.tanh(0.5 * x) + 0.5


def _silu(x):
    return x * _sigmoid(x)


def _rms(x, g):
    return x * lax.rsqrt(jnp.mean(x * x, axis=-1, keepdims=True) + EPS) * g


def _adaln_kernel(cond_ref, w_ref, b_ref, o_ref):
    s = _silu(cond_ref[...]).astype(BF16)
    o_ref[0] = jnp.dot(s, w_ref[0].astype(BF16), preferred_element_type=F32) + b_ref[0]


def _adaln(cond, w_ada, b_ada):
    depth, d, n = w_ada.shape
    r = cond.shape[0]
    tn = _pick(n, 1024)
    return pl.pallas_call(
        _adaln_kernel,
        grid=(depth, n // tn),
        in_specs=[pl.BlockSpec((r, d), lambda l, j: (0, 0)),
                  pl.BlockSpec((1, d, tn), lambda l, j: (l, 0, j)),
                  pl.BlockSpec((1, 1, tn), lambda l, j: (l, 0, j))],
        out_specs=pl.BlockSpec((1, r, tn), lambda l, j: (l, 0, j)),
        out_shape=jax.ShapeDtypeStruct((depth, r, n), F32),
        compiler_params=_cparams("parallel", "parallel"),
    )(cond, w_ada, b_ada.reshape(depth, 1, n))


def _norm_mod_kernel(x_ref, g_ref, shift_ref, scale_ref, o_ref):
    y = _rms(x_ref[...], g_ref[...])
    o_ref[...] = (y * (1.0 + scale_ref[0]) + shift_ref[0]).astype(o_ref.dtype)


def _norm_mod(x, g, mod3, row_of):
    t, d = x.shape
    tm = _pick(t, 512, 8)
    return pl.pallas_call(
        _norm_mod_kernel,
        grid=(t // tm,),
        in_specs=[pl.BlockSpec((tm, d), lambda i: (i, 0)),
                  pl.BlockSpec((1, d), lambda i: (0, 0)),
                  pl.BlockSpec((1, 1, d), lambda i: (row_of(i * tm), 0, 0)),
                  pl.BlockSpec((1, 1, d), lambda i: (row_of(i * tm), 0, 1))],
        out_specs=pl.BlockSpec((tm, d), lambda i: (i, 0)),
        out_shape=jax.ShapeDtypeStruct((t, d), BF16),
        compiler_params=_cparams("parallel"),
    )(x, g.reshape(1, d), mod3, mod3)


def _matmul_kernel(a_ref, w_ref, o_ref):
    o_ref[...] = jnp.dot(a_ref[...], w_ref[...], preferred_element_type=F32).astype(o_ref.dtype)


def _matmul(a, w, out_dtype=BF16):
    m, k = a.shape
    n = w.shape[1]
    bm, bn = _pick(m, 1024, 8), _pick(n, 1024)
    return pl.pallas_call(
        _matmul_kernel,
        grid=(m // bm, n // bn),
        in_specs=[pl.BlockSpec((bm, k), lambda i, j: (i, 0)),
                  pl.BlockSpec((k, bn), lambda i, j: (0, j))],
        out_specs=pl.BlockSpec((bm, bn), lambda i, j: (i, j)),
        out_shape=jax.ShapeDtypeStruct((m, n), out_dtype),
        compiler_params=_cparams("parallel", "parallel"),
    )(a, w)


def _out_proj_kernel(a1_ref, a2_ref, w1_ref, w2_ref, x_ref, gate_ref, o_ref):
    acc = jnp.dot(a1_ref[...], w1_ref[...].astype(BF16), preferred_element_type=F32)
    acc = acc + jnp.dot(a2_ref[...], w2_ref[...].astype(BF16), preferred_element_type=F32)
    o_ref[...] = x_ref[...] + gate_ref[0] * acc


def _out_proj(a1, a2, w, x, mod3, row_of):
    t, k1 = a1.shape
    d = w.shape[1]
    assert a2.shape == (t, k1) and w.shape[0] == 2 * k1
    bm, bn = _pick(t, 1024, 8), _pick(d, 512)
    gate_blk = 2 * (d // bn)
    return pl.pallas_call(
        _out_proj_kernel,
        grid=(t // bm, d // bn),
        in_specs=[pl.BlockSpec((bm, k1), lambda i, j: (i, 0)),
                  pl.BlockSpec((bm, k1), lambda i, j: (i, 0)),
                  pl.BlockSpec((k1, bn), lambda i, j: (0, j)),
                  pl.BlockSpec((k1, bn), lambda i, j: (1, j)),
                  pl.BlockSpec((bm, bn), lambda i, j: (i, j)),
                  pl.BlockSpec((1, 1, bn), lambda i, j: (row_of(i * bm), 0, gate_blk + j))],
        out_specs=pl.BlockSpec((bm, bn), lambda i, j: (i, j)),
        out_shape=jax.ShapeDtypeStruct((t, d), F32),
        compiler_params=_cparams("parallel", "parallel"),
    )(a1, a2, w, w, x, mod3)


def _dft_cos_sin(n):
    idx = np.arange(n, dtype=np.int64)
    ang = 2.0 * np.pi * ((idx[:, None] * idx[None, :]) % n) / n
    return np.cos(ang) / np.sqrt(n), np.sin(ang) / np.sqrt(n)


def _dft_fold_kernel(c_ref, s_ref, wf_ref, o_ref):
    cg = wf_ref.shape[1]
    hi = lax.Precision.HIGHEST
    o_ref[0, :, 0:cg] = jnp.dot(c_ref[...], wf_ref[0], precision=hi, preferred_element_type=F32).astype(BF16)
    o_ref[0, :, cg:] = jnp.dot(s_ref[...], wf_ref[0], precision=hi, preferred_element_type=F32).astype(BF16)


def _dft_fold(wf):
    groups, cg, _ = wf.shape
    cc, sc = _dft_cos_sin(cg)
    const = pl.BlockSpec((cg, cg), lambda g: (0, 0))
    return pl.pallas_call(
        _dft_fold_kernel,
        grid=(groups,),
        in_specs=[const, const, pl.BlockSpec((1, cg, cg), lambda g: (g, 0, 0))],
        out_specs=pl.BlockSpec((1, cg, 2 * cg), lambda g: (g, 0, 0)),
        out_shape=jax.ShapeDtypeStruct((groups, cg, 2 * cg), BF16),
        compiler_params=_cparams("parallel"),
    )(jnp.asarray(cc, F32), jnp.asarray(sc, F32), wf)


def _fourier_kernel(u_ref, gp_ref, cw_ref, csl_ref, o_ref, p_scr, *, seq, cg):
    n_seq = u_ref.shape[0] // seq
    tr = csl_ref.shape[0]

    @pl.when(pl.program_id(2) == 0)
    def _():
        for s in range(n_seq):
            p = jnp.dot(u_ref[s * seq:(s + 1) * seq, :], cw_ref[0], preferred_element_type=F32)
            p_scr[s, 0:seq, :] = p[:, :cg].astype(BF16)
            p_scr[s, seq:2 * seq, :] = p[:, cg:].astype(BF16)

    for s in range(n_seq):
        rows = slice(s * tr, (s + 1) * tr)
        y = jnp.dot(csl_ref[...], p_scr[s], preferred_element_type=F32)
        o_ref[rows, :] = (y * _silu(gp_ref[rows, :].astype(F32))).astype(o_ref.dtype)


def _fourier(u, cw, nb, seq, gp_col0):
    groups, cg, _ = cw.shape
    t = u.shape[0]
    tr = _pick(seq, 512, 8)
    rt = seq // tr
    spb = 1
    if rt == 1:
        spb = max(1, min(nb, FOURIER_BLOCK_ROWS // seq))
        while nb % spb:
            spb -= 1
    cl, sl = _dft_cos_sin(seq)
    csl = jnp.asarray(np.concatenate([cl, -sl], axis=1), BF16)
    gp_blk0 = gp_col0 // cg
    return pl.pallas_call(
        functools.partial(_fourier_kernel, seq=seq, cg=cg),
        grid=(nb // spb, groups, rt),
        in_specs=[pl.BlockSpec((spb * seq, cg), lambda b, g, r: (b, g)),
                  pl.BlockSpec((spb * tr, cg), lambda b, g, r: (b * rt + r, gp_blk0 + g)),
                  pl.BlockSpec((1, cg, 2 * cg), lambda b, g, r: (g, 0, 0)),
                  pl.BlockSpec((tr, 2 * seq), lambda b, g, r: (r, 0))],
        out_specs=pl.BlockSpec((spb * tr, cg), lambda b, g, r: (b * rt + r, g)),
        out_shape=jax.ShapeDtypeStruct((t, groups * cg), BF16),
        scratch_shapes=[pltpu.VMEM((spb, 2 * seq, cg), BF16)],
        compiler_params=_cparams("parallel", "parallel", "arbitrary"),
    )(u, u, cw, csl)


def _pool_tables(seq):
    r = np.arange(POOL_TILE)[:, None]
    s = np.arange(POOL_TILE + 2 * POOL_HALO)[None, :] - POOL_HALO
    tpos = np.arange(seq)
    bands, cnts = [], []
    for win in POOL_WINDOWS:
        half = win // 2
        assert half <= POOL_HALO
        bands.append(((s >= r - half) & (s <= r + half - 1)).astype(np.float32))
        cnts.append(np.minimum(tpos + half, seq) - np.maximum(tpos - half, 0))
    return np.stack(bands), np.stack(cnts).astype(np.float32)[:, :, None]


def _pool_kernel(u_ref, gp_ref, band_ref, cnt_ref, w_ref, ps_ref, o_ref, *, seq):
    tr = POOL_TILE
    for s0 in range(0, u_ref.shape[0], seq):
        for t in range(seq // tr):
            base = tr * t - POOL_HALO
            lo, hi = max(base, 0), min(base + tr + 2 * POOL_HALO, seq)
            rows = slice(s0 + tr * t, s0 + tr * (t + 1))
            wsum = jnp.dot(band_ref[0, :, lo - base:hi - base], u_ref[s0 + lo:s0 + hi, :],
                           preferred_element_type=F32)
            cnt = cnt_ref[0, tr * t:tr * (t + 1), :]
            d = (wsum / cnt - u_ref[rows, :].astype(F32)).astype(BF16)
            y = jnp.dot(d, w_ref[0], preferred_element_type=F32) * ps_ref[...]
            o_ref[rows, :] = (y * _silu(gp_ref[rows, :].astype(F32))).astype(o_ref.dtype)


def _pool(u, wp, pool_scale, nb, seq, in_col0, gp_col0):
    groups, cg, _ = wp.shape
    assert groups == len(POOL_WINDOWS) and seq % POOL_TILE == 0
    t = u.shape[0]
    band, cnt = _pool_tables(seq)
    in_blk0, gp_blk0 = in_col0 // cg, gp_col0 // cg
    spb = max(1, min(nb, POOL_BLOCK_ROWS // seq))
    while nb % spb:
        spb -= 1
    rows = spb * seq
    return pl.pallas_call(
        functools.partial(_pool_kernel, seq=seq),
        grid=(nb // spb, groups),
        in_specs=[pl.BlockSpec((rows, cg), lambda b, g: (b, in_blk0 + g)),
                  pl.BlockSpec((rows, cg), lambda b, g: (b, gp_blk0 + g)),
                  pl.BlockSpec((1,) + band.shape[1:], lambda b, g: (g, 0, 0)),
                  pl.BlockSpec((1, seq, 1), lambda b, g: (g, 0, 0)),
                  pl.BlockSpec((1, cg, cg), lambda b, g: (g, 0, 0)),
                  pl.BlockSpec((1, cg), lambda b, g: (0, g))],
        out_specs=pl.BlockSpec((rows, cg), lambda b, g: (b, g)),
        out_shape=jax.ShapeDtypeStruct((t, groups * cg), BF16),
        compiler_params=_cparams("parallel", "parallel"),
    )(u, u, jnp.asarray(band, BF16), jnp.asarray(cnt, F32), wp, pool_scale.reshape(1, groups * cg))


def _ctx_attn_kernel(q_ref, k_ref, v_ref, gp_ref, qg_ref, kg_ref, o_ref, sk_ref, sv_ref, *, heads, hd):
    qscale = hd ** -0.5 * LOG2E
    for h in range(heads):
        cols = slice(h * hd, (h + 1) * hd)
        qn = _rms(q_ref[:, cols].astype(F32), qg_ref[...]) * qscale
        kn = _rms(k_ref[:, cols].astype(F32), kg_ref[...])
        v = v_ref[:, cols]
        sk_ref[0, 0, h] = kn
        sv_ref[0, 0, h] = v.astype(F32)
        s = lax.dot_general(qn.astype(BF16), kn.astype(BF16), (((1,), (1,)), ((), ())),
                            preferred_element_type=F32)
        p = jnp.exp2(s - jnp.max(s, axis=-1, keepdims=True))
        o = jnp.dot(p.astype(BF16), v, preferred_element_type=F32) / jnp.sum(p, axis=-1, keepdims=True)
        o_ref[:, cols] = (o * _silu(gp_ref[:, cols].astype(F32))).astype(o_ref.dtype)


def _ctx_attn(u, qg, kg, nb, seq, d_attn, gp_col0):
    hd = qg.shape[-1]
    heads = d_attn // hd
    t = u.shape[0]
    state = jax.ShapeDtypeStruct((nb, 1, heads, seq, hd), F32)
    state_spec = pl.BlockSpec((1, 1, heads, seq, hd), lambda b: (b, 0, 0, 0, 0))
    gp_blk = gp_col0 // d_attn
    return pl.pallas_call(
        functools.partial(_ctx_attn_kernel, heads=heads, hd=hd),
        grid=(nb,),
        in_specs=[pl.BlockSpec((seq, d_attn), lambda b: (b, 0)),
                  pl.BlockSpec((seq, d_attn), lambda b: (b, 1)),
                  pl.BlockSpec((seq, d_attn), lambda b: (b, 2)),
                  pl.BlockSpec((seq, d_attn), lambda b: (b, gp_blk)),
                  pl.BlockSpec((1, hd), lambda b: (0, 0)),
                  pl.BlockSpec((1, hd), lambda b: (0, 0))],
        out_specs=[pl.BlockSpec((seq, d_attn), lambda b: (b, 0)), state_spec, state_spec],
        out_shape=[jax.ShapeDtypeStruct((t, d_attn), BF16), state, state],
        compiler_params=_cparams("parallel"),
    )(u, u, u, u, qg.reshape(1, hd), kg.reshape(1, hd))


def _nbr_plan(rows, kh, n_kh):
    assert rows % NA_QROWS == 0 and rows >= NA_KROWS
    n_dr = 2 * n_kh - 1
    blocks, types = [], []
    for r0 in range(0, rows, NA_QROWS):
        ws = int(np.clip(r0 - kh // 2, 0, rows - NA_KROWS))
        idx = np.full((NA_QROWS, NA_KROWS), n_dr, np.int32)
        for i in range(NA_QROWS):
            r = r0 + i
            start = int(np.clip(r - kh // 2, 0, rows - kh))
            assert ws <= start and start + kh <= ws + NA_KROWS
            for j in range(NA_KROWS):
                rk = ws + j
                if start <= rk < start + kh:
                    idx[i, j] = rk - r + n_kh - 1
        for t, known in enumerate(types):
            if np.array_equal(known, idx):
                break
        else:
            t = len(types)
            types.append(idx)
        blocks.append((r0, ws, t))
    return tuple(blocks), np.stack(types)


def _nbr_build_bias(rb_ref, bias_scr, head, blk_idx, n_dr, n_dc, kw):
    w = GRID_W
    c = lax.broadcasted_iota(jnp.int32, (w, 2 * w), 0)
    lane = lax.broadcasted_iota(jnp.int32, (w, 2 * w), 1)
    cp = lane % w
    q_start = jnp.clip(c - kw // 2, 0, w - kw)
    diff = jnp.where(cp >= q_start, jnp.where(cp < q_start + kw, cp - c + (kw - 1), -1), -1)
    neg = jnp.full((w, 2 * w), NEG_INF, F32)
    pairs = []
    for dr in range(n_dr):
        blk = neg
        for d in range(n_dc):
            blk = jnp.where(diff == d, rb_ref[(head * n_dr + dr) * n_dc + d] * LOG2E, blk)
        pairs.append(blk)
    pairs.append(neg)
    low = lane < w
    nt, qr, kr = blk_idx.shape
    for t in range(nt):
        for i in range(qr):
            for j in range(0, kr, 2):
                e, o = int(blk_idx[t, i, j]), int(blk_idx[t, i, j + 1])
                blk = pairs[e] if e == o else jnp.where(low, pairs[e], pairs[o])
                bias_scr[t, i * w:(i + 1) * w, j * w:(j + 2) * w] = blk


def _nbr_attn_kernel(rb_ref, q_ref, k_ref, v_ref, ck_ref, cv_ref, gp_ref, qg_ref, kg_ref, o_ref,
                     qn_scr, kn_scr, bias_scr, *, blocks, blk_idx, hd, kw, n_dr, n_dc):
    @pl.when(pl.program_id(1) == 0)
    def _():
        _nbr_build_bias(rb_ref, bias_scr, pl.program_id(0), blk_idx, n_dr, n_dc, kw)

    qscale = hd ** -0.5 * LOG2E
    qb, kb = NA_QROWS * GRID_W, NA_KROWS * GRID_W
    nt_dims = (((1,), (1,)), ((), ()))
    qn_scr[...] = (_rms(q_ref[...].astype(F32), qg_ref[...]) * qscale).astype(BF16)
    kn_scr[...] = _rms(k_ref[...].astype(F32), kg_ref[...]).astype(BF16)
    ck = ck_ref[0, 0, 0].astype(BF16)
    cv = cv_ref[0, 0, 0].astype(BF16)
    def logits(n):
        r0, ws, t = blocks[n]
        q = qn_scr[r0 * GRID_W:r0 * GRID_W + qb, :]
        s_loc = lax.dot_general(q, kn_scr[ws * GRID_W:ws * GRID_W + kb, :], nt_dims,
                                preferred_element_type=F32) + bias_scr[t]
        return s_loc, lax.dot_general(q, ck, nt_dims, preferred_element_type=F32)

    def softmax(s_loc, s_ctx):
        m = jnp.maximum(jnp.max(s_loc, axis=-1, keepdims=True), jnp.max(s_ctx, axis=-1, keepdims=True))
        p_loc = jnp.exp2(s_loc - m)
        p_ctx = jnp.exp2(s_ctx - m)
        denom = jnp.sum(p_loc, axis=-1, keepdims=True) + jnp.sum(p_ctx, axis=-1, keepdims=True)
        return p_loc.astype(BF16), p_ctx.astype(BF16), denom

    def attend(n, p_loc, p_ctx, denom):
        r0, ws, _ = blocks[n]
        qs = slice(r0 * GRID_W, r0 * GRID_W + qb)
        o = jnp.dot(p_loc, v_ref[ws * GRID_W:ws * GRID_W + kb, :], preferred_element_type=F32)
        o = (o + jnp.dot(p_ctx, cv, preferred_element_type=F32)) / denom
        o_ref[qs, :] = (o * _silu(gp_ref[qs, :].astype(F32))).astype(o_ref.dtype)

    s_cur, p_prev = logits(0), None
    for n in range(len(blocks)):
        p_cur = softmax(*s_cur)
        if p_prev is not None:
            attend(n - 1, *p_prev)
        if n + 1 < len(blocks):
            s_cur = logits(n + 1)
        p_prev = p_cur
    attend(len(blocks) - 1, *p_prev)


def _nbr_attn(u, cache_k, cache_v, layer, rel_bias, qg, kg, nb, seq, d_attn, gp_col0):
    hd = qg.shape[-1]
    heads = d_attn // hd
    t = u.shape[0]
    past = cache_k.shape[3]
    rows = seq // GRID_W
    _, n_dr, n_dc = rel_bias.shape
    n_kh, kw = (n_dr + 1) // 2, (n_dc + 1) // 2
    assert 2 * GRID_W == LANES and kw <= GRID_W
    blocks, blk_idx = _nbr_plan(rows, min(n_kh, rows), n_kh)
    gp_blk0 = gp_col0 // hd
    cache_spec = pl.BlockSpec((1, 1, 1, past, hd), lambda h, b: (b, layer, h, 0, 0))
    return pl.pallas_call(
        functools.partial(_nbr_attn_kernel, blocks=blocks, blk_idx=blk_idx, hd=hd, kw=kw, n_dr=n_dr,
                          n_dc=n_dc),
        grid=(heads, nb),
        in_specs=[pl.BlockSpec(memory_space=pltpu.SMEM),
                  pl.BlockSpec((seq, hd), lambda h, b: (b, h)),
                  pl.BlockSpec((seq, hd), lambda h, b: (b, heads + h)),
                  pl.BlockSpec((seq, hd), lambda h, b: (b, 2 * heads + h)),
                  cache_spec, cache_spec,
                  pl.BlockSpec((seq, hd), lambda h, b: (b, gp_blk0 + h)),
                  pl.BlockSpec((1, hd), lambda h, b: (0, 0)),
                  pl.BlockSpec((1, hd), lambda h, b: (0, 0))],
        out_specs=pl.BlockSpec((seq, hd), lambda h, b: (b, h)),
        out_shape=jax.ShapeDtypeStruct((t, d_attn), BF16),
        scratch_shapes=[pltpu.VMEM((seq, hd), BF16), pltpu.VMEM((seq, hd), BF16),
                        pltpu.VMEM((blk_idx.shape[0], NA_QROWS * GRID_W, NA_KROWS * GRID_W), F32)],
        compiler_params=_cparams("arbitrary", "arbitrary"),
    )(rel_bias.reshape(-1), u, u, u, cache_k, cache_v, u, qg.reshape(1, hd), kg.reshape(1, hd))


def _conv_kernel(a_ref, b_ref, ap_ref, bp_ref, an_ref, bn_ref, gp_ref, dw_ref, dwb_ref, lng_ref, lnb_ref,
                 wpw_ref, o_ref, h_scr, c_scr, z_scr, out_scr, *, tiles_per_seq, n_tiles, width):
    tm, dc = a_ref.shape
    nc = dc // LANES
    n_slabs, slab = wpw_ref.shape[0], wpw_ref.shape[2]
    per_slab = nc // n_slabs
    step = pl.program_id(0)

    @pl.when(step == 0)
    def _():
        z_scr[...] = jnp.zeros(z_scr.shape, z_scr.dtype)

    pos = jnp.minimum(step, n_tiles - 1) % tiles_per_seq
    keep_prev = (pos > 0).astype(F32)
    keep_next = (pos < tiles_per_seq - 1).astype(F32)
    for c in range(nc):
        cols = slice(c * LANES, (c + 1) * LANES)
        h_scr[c, 0:CONV_HALO, :] = keep_prev * (
            ap_ref[:, cols].astype(F32) * _sigmoid(bp_ref[:, cols].astype(F32)))
        h_scr[c, CONV_HALO:CONV_HALO + tm, :] = (
            a_ref[:, cols].astype(F32) * _sigmoid(b_ref[:, cols].astype(F32)))
        h_scr[c, CONV_HALO + tm:, :] = keep_next * (
            an_ref[:, cols].astype(F32) * _sigmoid(bn_ref[:, cols].astype(F32)))

    off = CONV_HALO - width // 2

    def slab_step(j, carry):
        for cc in range(per_slab):
            c = j * per_slab + cc
            for rb in range(tm // CONV_ROWS):
                acc = jnp.zeros((CONV_ROWS, LANES), F32)
                for k in range(width):
                    acc = acc + dw_ref[c, k:k + 1, :] * h_scr[c, pl.ds(rb * CONV_ROWS + off + k, CONV_ROWS), :]
                c_scr[c, rb * CONV_ROWS:(rb + 1) * CONV_ROWS, :] = acc + dwb_ref[c]
        out_scr[j] = jnp.dot(z_scr[...], wpw_ref[j], preferred_element_type=F32)
        return carry

    lax.fori_loop(0, n_slabs, slab_step, 0)

    for j in range(n_slabs):
        cols = slice(j * slab, (j + 1) * slab)
        o_ref[:, cols] = (out_scr[j] * _silu(gp_ref[:, cols].astype(F32))).astype(o_ref.dtype)

    s1 = jnp.zeros((tm, LANES), F32)
    for c in range(nc):
        s1 = s1 + c_scr[c]
    mu = jnp.sum(s1, axis=-1, keepdims=True) / dc
    s2 = jnp.zeros((tm, LANES), F32)
    for c in range(nc):
        xc = c_scr[c] - mu
        s2 = s2 + xc * xc
    rstd = lax.rsqrt(jnp.sum(s2, axis=-1, keepdims=True) / dc + EPS)
    for c in range(nc):
        cols = slice(c * LANES, (c + 1) * LANES)
        y = (c_scr[c] - mu) * rstd * lng_ref[:, cols] + lnb_ref[:, cols]
        z_scr[:, cols] = _silu(y).astype(BF16)


def _conv_module(u, dw, dw_b, ln_g, ln_b, w_pw, seq, in_col0, gp_col0):
    width, dc = dw.shape
    t = u.shape[0]
    tm = CONV_TILE
    assert seq % tm == 0 and width // 2 <= CONV_HALO and tm % CONV_HALO == 0 and dc % LANES == 0
    nc = dc // LANES
    tiles_per_seq = seq // tm
    hb = tm // CONV_HALO
    n_hblk = t // CONV_HALO
    a_blk, gp_blk = in_col0 // dc, gp_col0 // dc
    wpad = -width % 8
    dw3 = jnp.pad(dw, ((0, wpad), (0, 0))).reshape(width + wpad, nc, LANES).transpose(1, 0, 2)
    slab = _pick(dc, V7X_MXU_COLS)
    w_slabs = w_pw.reshape(dc, dc // slab, slab).transpose(1, 0, 2)
    n_tiles = t // tm
    tile = lambda i: jnp.minimum(i, n_tiles - 1)
    done = lambda i: jnp.maximum(i - 1, 0)
    prev = lambda i: jnp.maximum(tile(i) * hb - 1, 0)
    nxt = lambda i: jnp.minimum((tile(i) + 1) * hb, n_hblk - 1)
    vec = lambda x: x.reshape(1, dc)
    return pl.pallas_call(
        functools.partial(_conv_kernel, tiles_per_seq=tiles_per_seq, n_tiles=n_tiles, width=width),
        grid=(n_tiles + 1,),
        in_specs=[pl.BlockSpec((tm, dc), lambda i: (tile(i), a_blk)),
                  pl.BlockSpec((tm, dc), lambda i: (tile(i), a_blk + 1)),
                  pl.BlockSpec((CONV_HALO, dc), lambda i: (prev(i), a_blk)),
                  pl.BlockSpec((CONV_HALO, dc), lambda i: (prev(i), a_blk + 1)),
                  pl.BlockSpec((CONV_HALO, dc), lambda i: (nxt(i), a_blk)),
                  pl.BlockSpec((CONV_HALO, dc), lambda i: (nxt(i), a_blk + 1)),
                  pl.BlockSpec((tm, dc), lambda i: (done(i), gp_blk)),
                  pl.BlockSpec((nc, width + wpad, LANES), lambda i: (0, 0, 0)),
                  pl.BlockSpec((nc, 1, LANES), lambda i: (0, 0, 0)),
                  pl.BlockSpec((1, dc), lambda i: (0, 0)),
                  pl.BlockSpec((1, dc), lambda i: (0, 0)),
                  pl.BlockSpec((dc // slab, dc, slab), lambda i: (0, 0, 0))],
        out_specs=pl.BlockSpec((tm, dc), lambda i: (done(i), 0)),
        out_shape=jax.ShapeDtypeStruct((t, dc), BF16),
        scratch_shapes=[pltpu.VMEM((nc, tm + 2 * CONV_HALO, LANES), F32),
                        pltpu.VMEM((nc, tm, LANES), F32),
                        pltpu.VMEM((tm, dc), BF16),
                        pltpu.VMEM((dc // slab, tm, slab), F32)],
        compiler_params=_cparams("arbitrary"),
    )(u, u, u, u, u, u, u, dw3, dw_b.reshape(nc, 1, LANES), vec(ln_g), vec(ln_b), w_slabs)


def _trunk(x, nb, seq, row_of, mod, caches, wts):
    d = x.shape[1]
    depth = mod.shape[0]
    new_k, new_v = [], []
    for i in range(depth):
        mod3 = mod[i][:, None, :]
        h = _norm_mod(x, wts["norm_g"][i], mod3, row_of)
        j = i // 2
        if i % 2 == 0:
            u = _matmul(h, wts["w_in_even"][j])
            d_f = wts["w_fourier"].shape[1] * wts["w_fourier"].shape[2]
            d_inner = d_f + wts["w_pool"].shape[1] * wts["w_pool"].shape[2]
            y1 = _fourier(u, wts["dft_w"][j], nb, seq, gp_col0=d_inner)
            y2 = _pool(u, wts["w_pool"][j], wts["pool_scale"][j], nb, seq, in_col0=d_f,
                       gp_col0=d_inner + d_f)
            x = _out_proj(y1, y2, wts["w_out_even"][j], x, mod3, row_of)
        else:
            u = _matmul(h, wts["w_in_odd"][j])
            d_conv = wts["conv_dw"].shape[2]
            d_attn = (u.shape[1] - 3 * d_conv) // 4
            gp_col0 = 3 * d_attn + 2 * d_conv
            if caches is None:
                y1, sk, sv = _ctx_attn(u, wts["q_norm_g"][j], wts["k_norm_g"][j], nb, seq, d_attn, gp_col0)
                new_k.append(sk)
                new_v.append(sv)
            else:
                y1 = _nbr_attn(u, caches[0], caches[1], j, wts["rel_bias"][j], wts["q_norm_g"][j],
                               wts["k_norm_g"][j], nb, seq, d_attn, gp_col0)
            y2 = _conv_module(u, wts["conv_dw"][j], wts["conv_dw_b"][j], wts["conv_ln_g"][j],
                              wts["conv_ln_b"][j], wts["w_conv_pw"][j], seq, in_col0=3 * d_attn,
                              gp_col0=gp_col0 + d_attn)
            x = _out_proj(y1, y2, wts["w_out_odd"][j], x, mod3, row_of)
    return x, new_k, new_v


def kernel(x_prompt, x_sample, cache_k, cache_v, c, c_ctx, norm_g, w_ada, b_ada, w_in_even, w_out_even,
           w_fourier, w_pool, pool_scale, w_in_odd, w_out_odd, q_norm_g, k_norm_g, rel_bias, conv_dw,
           conv_dw_b, conv_ln_g, conv_ln_b, w_conv_pw):
    nb_p, seq_p, d = x_prompt.shape
    nb_s, seq_s, _ = x_sample.shape
    n_rows = -(-(1 + nb_s) // 8) * 8
    cond = jnp.concatenate([c_ctx[None, :], c, jnp.zeros((n_rows - 1 - nb_s, d), F32)], axis=0)
    mod = _adaln(cond, w_ada, b_ada)
    wts = dict(norm_g=norm_g, pool_scale=pool_scale, q_norm_g=q_norm_g, k_norm_g=k_norm_g,
               rel_bias=rel_bias, conv_dw=conv_dw, conv_dw_b=conv_dw_b, conv_ln_g=conv_ln_g,
               conv_ln_b=conv_ln_b, w_out_even=w_out_even, w_out_odd=w_out_odd)
    wts["w_fourier"] = w_fourier
    wts["dft_w"] = [_dft_fold(w_fourier[j]) for j in range(w_fourier.shape[0])]
    for name, w in (("w_in_even", w_in_even), ("w_pool", w_pool), ("w_in_odd", w_in_odd),
                    ("w_conv_pw", w_conv_pw)):
        wts[name] = w.astype(BF16)
    y_p, ks, vs = _trunk(x_prompt.reshape(nb_p * seq_p, d), nb_p, seq_p, lambda r: 0, mod, None, wts)
    y_s, _, _ = _trunk(x_sample.reshape(nb_s * seq_s, d), nb_s, seq_s, lambda r: 1 + r // seq_s, mod,
                       (cache_k, cache_v), wts)
    state_k = jnp.concatenate(ks, axis=1)
    state_v = jnp.concatenate(vs, axis=1)
    return (y_p.reshape(nb_p, seq_p, d), y_s.reshape(nb_s, seq_s, d), state_k, state_v)
```

```python
import functools

import numpy as np
import jax
import jax.numpy as jnp
from jax import lax
from jax.experimental import pallas as pl
from jax.experimental.pallas import tpu as pltpu

F32 = jnp.float32
BF16 = jnp.bfloat16

GRID_W = 64
POOL_WINDOWS = (2, 4, 8, 16)
EPS = 1e-6
NEG_INF = -1e30
LOG2E = float(np.log2(np.e))

LANES = 128
V7X_VMEM_LIMIT_BYTES = 56 * 1024 * 1024

NORM_ROWS = 16

NA_QROWS = 8
NA_KROWS = 16
POOL_TILE = 256
POOL_HALO = 128
POOL_BLOCK_ROWS = 2048
FOURIER_BLOCK_ROWS = 1024
CONV_TILE = 256
CONV_HALO = 16
CONV_ROWS = 64


def _cparams(*sem):
    return pltpu.CompilerParams(dimension_semantics=sem, vmem_limit_bytes=V7X_VMEM_LIMIT_BYTES)


def _pick(n, pref, unit=LANES):
    if n <= pref:
        return n
    t = (pref // unit) * unit
    while t > unit and n % t:
        t -= unit
    assert n % t == 0, (n, pref, unit)
    return t


def _sigmoid(x):
    return 0.5 * jnp.tanh(0.5 * x) + 0.5


def _silu(x):
    return x * _sigmoid(x)


def _rms(x, g):
    return x * lax.rsqrt(jnp.mean(x * x, axis=-1, keepdims=True) + EPS) * g


def _adaln_kernel(cond_ref, w_ref, b_ref, o_ref):
    s = _silu(cond_ref[...]).astype(BF16)
    o_ref[0] = jnp.dot(s, w_ref[0].astype(BF16), preferred_element_type=F32) + b_ref[0]


def _adaln(cond, w_ada, b_ada):
    depth, d, n = w_ada.shape
    r = cond.shape[0]
    tn = _pick(n, 1024)
    return pl.pallas_call(
        _adaln_kernel,
        grid=(depth, n // tn),
        in_specs=[pl.BlockSpec((r, d), lambda l, j: (0, 0)),
                  pl.BlockSpec((1, d, tn), lambda l, j: (l, 0, j)),
                  pl.BlockSpec((1, 1, tn), lambda l, j: (l, 0, j))],
        out_specs=pl.BlockSpec((1, r, tn), lambda l, j: (l, 0, j)),
        out_shape=jax.ShapeDtypeStruct((depth, r, n), F32),
        compiler_params=_cparams("parallel", "parallel"),
    )(cond, w_ada, b_ada.reshape(depth, 1, n))


def _norm_mod_kernel(x_ref, g_ref, shift_ref, scale_ref, o_ref, gain_scr, shift_scr):
    gain_scr[...] = jnp.broadcast_to(g_ref[...] * (1.0 + scale_ref[0]), gain_scr.shape)
    shift_scr[...] = jnp.broadcast_to(shift_ref[0], shift_scr.shape)

    def rows16(r, carry):
        rows = pl.ds(pl.multiple_of(r * NORM_ROWS, NORM_ROWS), NORM_ROWS)
        x = x_ref[rows, :]
        rstd = lax.rsqrt(jnp.mean(x * x, axis=-1, keepdims=True) + EPS)
        o_ref[rows, :] = (x * rstd * gain_scr[...] + shift_scr[...]).astype(o_ref.dtype)
        return carry

    lax.fori_loop(0, x_ref.shape[0] // NORM_ROWS, rows16, 0, unroll=4)


def _norm_mod(x, g, mod3, row_of):
    t, d = x.shape
    tm = _pick(t, 512, 8)
    return pl.pallas_call(
        _norm_mod_kernel,
        grid=(t // tm,),
        in_specs=[pl.BlockSpec((tm, d), lambda i: (i, 0)),
                  pl.BlockSpec((1, d), lambda i: (0, 0)),
                  pl.BlockSpec((1, 1, d), lambda i: (row_of(i * tm), 0, 0)),
                  pl.BlockSpec((1, 1, d), lambda i: (row_of(i * tm), 0, 1))],
        out_specs=pl.BlockSpec((tm, d), lambda i: (i, 0)),
        out_shape=jax.ShapeDtypeStruct((t, d), BF16),
        scratch_shapes=[pltpu.VMEM((NORM_ROWS, d), F32), pltpu.VMEM((NORM_ROWS, d), F32)],
        compiler_params=_cparams("parallel"),
    )(x, g.reshape(1, d), mod3, mod3)


def _side_casts(side, gi, gj):
    specs, shapes = [], []
    for s in side:
        rows, cols = s.shape
        nblk = min(gi * gj, rows // 16)
        while rows % nblk or (rows // nblk) % 16:
            nblk -= 1
        specs.append(pl.BlockSpec((rows // nblk, cols),
                                  lambda i, j, nblk=nblk: (jnp.minimum(i * gj + j, nblk - 1), 0)))
        shapes.append(jax.ShapeDtypeStruct(s.shape, BF16))
    return specs, shapes


def _cast_side(side_refs):
    n = len(side_refs) // 2
    for src_ref, dst_ref in zip(side_refs[:n], side_refs[n:]):
        dst_ref[...] = src_ref[...].astype(dst_ref.dtype)


def _matmul_kernel(a_ref, w_ref, *refs):
    n_side = (len(refs) - 1) // 2
    o_ref = refs[n_side]
    o_ref[...] = jnp.dot(a_ref[...], w_ref[...], preferred_element_type=F32).astype(o_ref.dtype)
    _cast_side(refs[:n_side] + refs[n_side + 1:])


def _matmul(a, w, side=()):
    m, k = a.shape
    n = w.shape[1]
    bm, bn = _pick(m, 1024, 8), _pick(n, 1024)
    gi, gj = m // bm, n // bn
    side_specs, side_shapes = _side_casts(side, gi, gj)
    outs = pl.pallas_call(
        _matmul_kernel,
        grid=(gi, gj),
        in_specs=[pl.BlockSpec((bm, k), lambda i, j: (i, 0)),
                  pl.BlockSpec((k, bn), lambda i, j: (0, j))] + side_specs,
        out_specs=[pl.BlockSpec((bm, bn), lambda i, j: (i, j))] + side_specs,
        out_shape=[jax.ShapeDtypeStruct((m, n), BF16)] + side_shapes,
        compiler_params=_cparams("arbitrary", "arbitrary"),
    )(a, w, *side)
    return outs[0], list(outs[1:])


def _out_proj_kernel(a1_ref, a2_ref, w1_ref, w2_ref, x_ref, gate_ref, *refs):
    n_side = (len(refs) - 1) // 2
    o_ref = refs[n_side]
    acc = jnp.dot(a1_ref[...], w1_ref[...], preferred_element_type=F32)
    acc = acc + jnp.dot(a2_ref[...], w2_ref[...], preferred_element_type=F32)
    o_ref[...] = x_ref[...] + gate_ref[0] * acc
    _cast_side(refs[:n_side] + refs[n_side + 1:])


def _out_proj(a1, a2, w, x, mod3, row_of, side=()):
    t, k1 = a1.shape
    d = w.shape[1]
    assert a2.shape == (t, k1) and w.shape[0] == 2 * k1
    bm, bn = _pick(t, 1024, 8), _pick(d, 512)
    gi, gj = t // bm, d // bn
    gate_blk = 2 * gj
    side_specs, side_shapes = _side_casts(side, gi, gj)
    outs = pl.pallas_call(
        _out_proj_kernel,
        grid=(gi, gj),
        in_specs=[pl.BlockSpec((bm, k1), lambda i, j: (i, 0)),
                  pl.BlockSpec((bm, k1), lambda i, j: (i, 0)),
                  pl.BlockSpec((k1, bn), lambda i, j: (0, j)),
                  pl.BlockSpec((k1, bn), lambda i, j: (1, j)),
                  pl.BlockSpec((bm, bn), lambda i, j: (i, j)),
                  pl.BlockSpec((1, 1, bn), lambda i, j: (row_of(i * bm), 0, gate_blk + j))] + side_specs,
        out_specs=[pl.BlockSpec((bm, bn), lambda i, j: (i, j))] + side_specs,
        out_shape=[jax.ShapeDtypeStruct((t, d), F32)] + side_shapes,
        compiler_params=_cparams("arbitrary", "arbitrary"),
    )(a1, a2, w, w, x, mod3, *side)
    return outs[0], list(outs[1:])


def _dft_cos_sin(n):
    idx = np.arange(n, dtype=np.int64)
    ang = 2.0 * np.pi * ((idx[:, None] * idx[None, :]) % n) / n
    return np.cos(ang) / np.sqrt(n), np.sin(ang) / np.sqrt(n)


def _dft_fold_kernel(c_ref, s_ref, wf_ref, o_ref):
    cg = wf_ref.shape[1]
    hi = lax.Precision.HIGHEST
    o_ref[0, :, 0:cg] = jnp.dot(c_ref[...], wf_ref[0], precision=hi, preferred_element_type=F32).astype(BF16)
    o_ref[0, :, cg:] = jnp.dot(s_ref[...], wf_ref[0], precision=hi, preferred_element_type=F32).astype(BF16)


def _dft_fold(wf):
    groups, cg, _ = wf.shape
    cc, sc = _dft_cos_sin(cg)
    const = pl.BlockSpec((cg, cg), lambda g: (0, 0))
    return pl.pallas_call(
        _dft_fold_kernel,
        grid=(groups,),
        in_specs=[const, const, pl.BlockSpec((1, cg, cg), lambda g: (g, 0, 0))],
        out_specs=pl.BlockSpec((1, cg, 2 * cg), lambda g: (g, 0, 0)),
        out_shape=jax.ShapeDtypeStruct((groups, cg, 2 * cg), BF16),
        compiler_params=_cparams("parallel"),
    )(jnp.asarray(cc, F32), jnp.asarray(sc, F32), wf)


def _fourier_kernel(u_ref, gp_ref, cw_ref, csl_ref, o_ref, p_scr, *, seq, cg):
    n_seq = u_ref.shape[0] // seq
    tr = csl_ref.shape[0]

    @pl.when(pl.program_id(2) == 0)
    def _():
        for s in range(n_seq):
            p = jnp.dot(u_ref[s * seq:(s + 1) * seq, :], cw_ref[0], preferred_element_type=F32)
            p_scr[s, 0:seq, :] = p[:, :cg].astype(BF16)
            p_scr[s, seq:2 * seq, :] = p[:, cg:].astype(BF16)

    for s in range(n_seq):
        rows = slice(s * tr, (s + 1) * tr)
        y = jnp.dot(csl_ref[...], p_scr[s], preferred_element_type=F32)
        o_ref[rows, :] = (y * _silu(gp_ref[rows, :].astype(F32))).astype(o_ref.dtype)


def _fourier(u, cw, nb, seq, gp_col0):
    groups, cg, _ = cw.shape
    t = u.shape[0]
    tr = _pick(seq, 512, 8)
    rt = seq // tr
    spb = 1
    if rt == 1:
        spb = max(1, min(nb, FOURIER_BLOCK_ROWS // seq))
        while nb % spb:
            spb -= 1
    cl, sl = _dft_cos_sin(seq)
    csl = jnp.asarray(np.concatenate([cl, -sl], axis=1), BF16)
    gp_blk0 = gp_col0 // cg
    return pl.pallas_call(
        functools.partial(_fourier_kernel, seq=seq, cg=cg),
        grid=(nb // spb, groups, rt),
        in_specs=[pl.BlockSpec((spb * seq, cg), lambda b, g, r: (b, g)),
                  pl.BlockSpec((spb * tr, cg), lambda b, g, r: (b * rt + r, gp_blk0 + g)),
                  pl.BlockSpec((1, cg, 2 * cg), lambda b, g, r: (g, 0, 0)),
                  pl.BlockSpec((tr, 2 * seq), lambda b, g, r: (r, 0))],
        out_specs=pl.BlockSpec((spb * tr, cg), lambda b, g, r: (b * rt + r, g)),
        out_shape=jax.ShapeDtypeStruct((t, groups * cg), BF16),
        scratch_shapes=[pltpu.VMEM((spb, 2 * seq, cg), BF16)],
        compiler_params=_cparams("parallel", "parallel", "arbitrary"),
    )(u, u, cw, csl)


def _pool_tables(seq):
    r = np.arange(POOL_TILE)[:, None]
    s = np.arange(POOL_TILE + 2 * POOL_HALO)[None, :] - POOL_HALO
    tpos = np.arange(seq)
    bands, cnts = [], []
    for win in POOL_WINDOWS:
        half = win // 2
        assert half <= POOL_HALO
        bands.append(((s >= r - half) & (s <= r + half - 1)).astype(np.float32))
        cnts.append(np.minimum(tpos + half, seq) - np.maximum(tpos - half, 0))
    return np.stack(bands), np.stack(cnts).astype(np.float32)[:, :, None]


def _pool_kernel(u_ref, gp_ref, band_ref, cnt_ref, w_ref, ps_ref, o_ref, *, seq):
    tr = POOL_TILE
    for s0 in range(0, u_ref.shape[0], seq):
        for t in range(seq // tr):
            base = tr * t - POOL_HALO
            lo, hi = max(base, 0), min(base + tr + 2 * POOL_HALO, seq)
            rows = slice(s0 + tr * t, s0 + tr * (t + 1))
            wsum = jnp.dot(band_ref[0, :, lo - base:hi - base], u_ref[s0 + lo:s0 + hi, :],
                           preferred_element_type=F32)
            cnt = cnt_ref[0, tr * t:tr * (t + 1), :]
            d = (wsum / cnt - u_ref[rows, :].astype(F32)).astype(BF16)
            y = jnp.dot(d, w_ref[0], preferred_element_type=F32) * ps_ref[...]
            o_ref[rows, :] = (y * _silu(gp_ref[rows, :].astype(F32))).astype(o_ref.dtype)


def _pool(u, wp, pool_scale, nb, seq, in_col0, gp_col0):
    groups, cg, _ = wp.shape
    assert groups == len(POOL_WINDOWS) and seq % POOL_TILE == 0
    t = u.shape[0]
    band, cnt = _pool_tables(seq)
    in_blk0, gp_blk0 = in_col0 // cg, gp_col0 // cg
    spb = max(1, min(nb, POOL_BLOCK_ROWS // seq))
    while nb % spb:
        spb -= 1
    rows = spb * seq
    return pl.pallas_call(
        functools.partial(_pool_kernel, seq=seq),
        grid=(nb // spb, groups),
        in_specs=[pl.BlockSpec((rows, cg), lambda b, g: (b, in_blk0 + g)),
                  pl.BlockSpec((rows, cg), lambda b, g: (b, gp_blk0 + g)),
                  pl.BlockSpec((1,) + band.shape[1:], lambda b, g: (g, 0, 0)),
                  pl.BlockSpec((1, seq, 1), lambda b, g: (g, 0, 0)),
                  pl.BlockSpec((1, cg, cg), lambda b, g: (g, 0, 0)),
                  pl.BlockSpec((1, cg), lambda b, g: (0, g))],
        out_specs=pl.BlockSpec((rows, cg), lambda b, g: (b, g)),
        out_shape=jax.ShapeDtypeStruct((t, groups * cg), BF16),
        compiler_params=_cparams("parallel", "parallel"),
    )(u, u, jnp.asarray(band, BF16), jnp.asarray(cnt, F32), wp, pool_scale.reshape(1, groups * cg))


def _ctx_attn_kernel(q_ref, k_ref, v_ref, gp_ref, qg_ref, kg_ref, o_ref, sk_ref, sv_ref, *, heads, hd):
    qscale = hd ** -0.5 * LOG2E
    for h in range(heads):
        cols = slice(h * hd, (h + 1) * hd)
        qn = _rms(q_ref[:, cols].astype(F32), qg_ref[...]) * qscale
        kn = _rms(k_ref[:, cols].astype(F32), kg_ref[...])
        v = v_ref[:, cols]
        sk_ref[0, 0, h] = kn
        sv_ref[0, 0, h] = v.astype(F32)
        s = lax.dot_general(qn.astype(BF16), kn.astype(BF16), (((1,), (1,)), ((), ())),
                            preferred_element_type=F32)
        p = jnp.exp2(s - jnp.max(s, axis=-1, keepdims=True))
        v_ones = jnp.concatenate([v, jnp.ones_like(v)], axis=1)
        o = jnp.dot(p.astype(BF16), v_ones, preferred_element_type=F32)
        o = o[:, :hd] / o[:, hd:]
        o_ref[:, cols] = (o * _silu(gp_ref[:, cols].astype(F32))).astype(o_ref.dtype)


def _ctx_attn(u, qg, kg, nb, seq, d_attn, gp_col0):
    hd = qg.shape[-1]
    heads = d_attn // hd
    t = u.shape[0]
    state = jax.ShapeDtypeStruct((nb, 1, heads, seq, hd), F32)
    state_spec = pl.BlockSpec((1, 1, heads, seq, hd), lambda b: (b, 0, 0, 0, 0))
    gp_blk = gp_col0 // d_attn
    return pl.pallas_call(
        functools.partial(_ctx_attn_kernel, heads=heads, hd=hd),
        grid=(nb,),
        in_specs=[pl.BlockSpec((seq, d_attn), lambda b: (b, 0)),
                  pl.BlockSpec((seq, d_attn), lambda b: (b, 1)),
                  pl.BlockSpec((seq, d_attn), lambda b: (b, 2)),
                  pl.BlockSpec((seq, d_attn), lambda b: (b, gp_blk)),
                  pl.BlockSpec((1, hd), lambda b: (0, 0)),
                  pl.BlockSpec((1, hd), lambda b: (0, 0))],
        out_specs=[pl.BlockSpec((seq, d_attn), lambda b: (b, 0)), state_spec, state_spec],
        out_shape=[jax.ShapeDtypeStruct((t, d_attn), BF16), state, state],
        compiler_params=_cparams("parallel"),
    )(u, u, u, u, qg.reshape(1, hd), kg.reshape(1, hd))


def _nbr_plan(rows, kh, n_kh):
    assert rows % NA_QROWS == 0 and rows >= NA_KROWS
    n_dr = 2 * n_kh - 1
    blocks, types = [], []
    for r0 in range(0, rows, NA_QROWS):
        ws = int(np.clip(r0 - kh // 2, 0, rows - NA_KROWS))
        idx = np.full((NA_QROWS, NA_KROWS), n_dr, np.int32)
        for i in range(NA_QROWS):
            r = r0 + i
            start = int(np.clip(r - kh // 2, 0, rows - kh))
            assert ws <= start and start + kh <= ws + NA_KROWS
            for j in range(NA_KROWS):
                rk = ws + j
                if start <= rk < start + kh:
                    idx[i, j] = rk - r + n_kh - 1
        for t, known in enumerate(types):
            if np.array_equal(known, idx):
                break
        else:
            t = len(types)
            types.append(idx)
        blocks.append((r0, ws, t))
    return tuple(blocks), np.stack(types)


def _nbr_build_bias(rb_ref, bias_scr, head, blk_idx, n_dr, n_dc, kw):
    w = GRID_W
    c = lax.broadcasted_iota(jnp.int32, (w, 2 * w), 0)
    lane = lax.broadcasted_iota(jnp.int32, (w, 2 * w), 1)
    cp = lane % w
    q_start = jnp.clip(c - kw // 2, 0, w - kw)
    diff = jnp.where(cp >= q_start, jnp.where(cp < q_start + kw, cp - c + (kw - 1), -1), -1)
    neg = jnp.full((w, 2 * w), NEG_INF, F32)
    pairs = []
    for dr in range(n_dr):
        blk = neg
        for d in range(n_dc):
            blk = jnp.where(diff == d, rb_ref[(head * n_dr + dr) * n_dc + d] * LOG2E, blk)
        pairs.append(blk)
    pairs.append(neg)
    low = lane < w
    nt, qr, kr = blk_idx.shape
    for t in range(nt):
        for i in range(qr):
            for j in range(0, kr, 2):
                e, o = int(blk_idx[t, i, j]), int(blk_idx[t, i, j + 1])
                blk = pairs[e] if e == o else jnp.where(low, pairs[e], pairs[o])
                bias_scr[t, i * w:(i + 1) * w, j * w:(j + 2) * w] = blk


def _nbr_attn_kernel(rb_ref, q_ref, k_ref, v_ref, ck_ref, cv_ref, gp_ref, qg_ref, kg_ref, o_ref,
                     qn_scr, kn_scr, v1_scr, bias_scr, *, blocks, blk_idx, hd, kw, n_dr, n_dc):
    @pl.when(pl.program_id(1) == 0)
    def _():
        _nbr_build_bias(rb_ref, bias_scr, pl.program_id(0), blk_idx, n_dr, n_dc, kw)

    qscale = hd ** -0.5 * LOG2E
    qb, kb = NA_QROWS * GRID_W, NA_KROWS * GRID_W
    nt_dims = (((1,), (1,)), ((), ()))
    qn_scr[...] = (_rms(q_ref[...].astype(F32), qg_ref[...]) * qscale).astype(BF16)
    kn_scr[...] = _rms(k_ref[...].astype(F32), kg_ref[...]).astype(BF16)
    ck = ck_ref[0, 0, 0].astype(BF16)
    cv = cv_ref[0, 0, 0].astype(BF16)
    cv1 = jnp.concatenate([cv, jnp.ones_like(cv)], axis=1)
    v1_scr[:, 0:hd] = v_ref[...]
    v1_scr[:, hd:] = jnp.ones(v_ref.shape, BF16)

    def logits(n):
        r0, ws, t = blocks[n]
        q = qn_scr[r0 * GRID_W:r0 * GRID_W + qb, :]
        s_loc = lax.dot_general(q, kn_scr[ws * GRID_W:ws * GRID_W + kb, :], nt_dims,
                                preferred_element_type=F32) + bias_scr[t]
        return s_loc, lax.dot_general(q, ck, nt_dims, preferred_element_type=F32)

    def softmax(s_loc, s_ctx):
        m = jnp.maximum(jnp.max(s_loc, axis=-1, keepdims=True), jnp.max(s_ctx, axis=-1, keepdims=True))
        p_loc = jnp.exp2(s_loc - m)
        p_ctx = jnp.exp2(s_ctx - m)
        return p_loc.astype(BF16), p_ctx.astype(BF16)

    def attend(n, p_loc, p_ctx):
        r0, ws, _ = blocks[n]
        qs = slice(r0 * GRID_W, r0 * GRID_W + qb)
        o = jnp.dot(p_loc, v1_scr[ws * GRID_W:ws * GRID_W + kb, :], preferred_element_type=F32)
        o = o + jnp.dot(p_ctx, cv1, preferred_element_type=F32)
        o = o[:, :hd] / o[:, hd:]
        o_ref[qs, :] = (o * _silu(gp_ref[qs, :].astype(F32))).astype(o_ref.dtype)

    s_cur, p_prev = logits(0), None
    for n in range(len(blocks)):
        p_cur = softmax(*s_cur)
        if p_prev is not None:
            attend(n - 1, *p_prev)
        if n + 1 < len(blocks):
            s_cur = logits(n + 1)
        p_prev = p_cur
    attend(len(blocks) - 1, *p_prev)


def _nbr_attn(u, cache_k, cache_v, layer, rel_bias, qg, kg, nb, seq, d_attn, gp_col0):
    hd = qg.shape[-1]
    heads = d_attn // hd
    t = u.shape[0]
    past = cache_k.shape[3]
    rows = seq // GRID_W
    _, n_dr, n_dc = rel_bias.shape
    n_kh, kw = (n_dr + 1) // 2, (n_dc + 1) // 2
    assert 2 * GRID_W == LANES and kw <= GRID_W
    blocks, blk_idx = _nbr_plan(rows, min(n_kh, rows), n_kh)
    gp_blk0 = gp_col0 // hd
    cache_spec = pl.BlockSpec((1, 1, 1, past, hd), lambda h, b: (b, layer, h, 0, 0))
    return pl.pallas_call(
        functools.partial(_nbr_attn_kernel, blocks=blocks, blk_idx=blk_idx, hd=hd, kw=kw, n_dr=n_dr,
                          n_dc=n_dc),
        grid=(heads, nb),
        in_specs=[pl.BlockSpec(memory_space=pltpu.SMEM),
                  pl.BlockSpec((seq, hd), lambda h, b: (b, h)),
                  pl.BlockSpec((seq, hd), lambda h, b: (b, heads + h)),
                  pl.BlockSpec((seq, hd), lambda h, b: (b, 2 * heads + h)),
                  cache_spec, cache_spec,
                  pl.BlockSpec((seq, hd), lambda h, b: (b, gp_blk0 + h)),
                  pl.BlockSpec((1, hd), lambda h, b: (0, 0)),
                  pl.BlockSpec((1, hd), lambda h, b: (0, 0))],
        out_specs=pl.BlockSpec((seq, hd), lambda h, b: (b, h)),
        out_shape=jax.ShapeDtypeStruct((t, d_attn), BF16),
        scratch_shapes=[pltpu.VMEM((seq, hd), BF16), pltpu.VMEM((seq, hd), BF16),
                        pltpu.VMEM((seq, 2 * hd), BF16),
                        pltpu.VMEM((blk_idx.shape[0], NA_QROWS * GRID_W, NA_KROWS * GRID_W), F32)],
        compiler_params=_cparams("arbitrary", "arbitrary"),
    )(rel_bias.reshape(-1), u, u, u, cache_k, cache_v, u, qg.reshape(1, hd), kg.reshape(1, hd))


def _conv_kernel(a_ref, b_ref, ap_ref, bp_ref, an_ref, bn_ref, gp_ref, dw_ref, dwb_ref, lng_ref, lnb_ref,
                 wpw_ref, o_ref, h_scr, c_scr, z_scr, *, tiles_per_seq, width):
    tm, dc = a_ref.shape
    nc = dc // LANES
    pos = pl.program_id(0) % tiles_per_seq
    keep_prev = (pos > 0).astype(F32)
    keep_next = (pos < tiles_per_seq - 1).astype(F32)
    for c in range(nc):
        cols = slice(c * LANES, (c + 1) * LANES)
        h_scr[c, 0:CONV_HALO, :] = keep_prev * (
            ap_ref[:, cols].astype(F32) * _sigmoid(bp_ref[:, cols].astype(F32)))
        h_scr[c, CONV_HALO:CONV_HALO + tm, :] = (
            a_ref[:, cols].astype(F32) * _sigmoid(b_ref[:, cols].astype(F32)))
        h_scr[c, CONV_HALO + tm:, :] = keep_next * (
            an_ref[:, cols].astype(F32) * _sigmoid(bn_ref[:, cols].astype(F32)))

    off = CONV_HALO - width // 2

    def chunk(c, carry):
        for rb in range(tm // CONV_ROWS):
            acc = jnp.zeros((CONV_ROWS, LANES), F32)
            for k in range(width):
                acc = acc + dw_ref[c, k:k + 1, :] * h_scr[c, pl.ds(rb * CONV_ROWS + off + k, CONV_ROWS), :]
            c_scr[c, rb * CONV_ROWS:(rb + 1) * CONV_ROWS, :] = acc + dwb_ref[c]
        return carry

    lax.fori_loop(0, nc, chunk, 0)

    s1 = jnp.zeros((tm, LANES), F32)
    for c in range(nc):
        s1 = s1 + c_scr[c]
    mu = jnp.sum(s1, axis=-1, keepdims=True) / dc
    s2 = jnp.zeros((tm, LANES), F32)
    for c in range(nc):
        xc = c_scr[c] - mu
        s2 = s2 + xc * xc
    rstd = lax.rsqrt(jnp.sum(s2, axis=-1, keepdims=True) / dc + EPS)
    for c in range(nc):
        cols = slice(c * LANES, (c + 1) * LANES)
        y = (c_scr[c] - mu) * rstd * lng_ref[:, cols] + lnb_ref[:, cols]
        z_scr[:, cols] = _silu(y).astype(BF16)
    out = jnp.dot(z_scr[...], wpw_ref[...], preferred_element_type=F32)
    o_ref[...] = (out * _silu(gp_ref[...].astype(F32))).astype(o_ref.dtype)


def _conv_module(u, dw, dw_b, ln_g, ln_b, w_pw, seq, in_col0, gp_col0):
    width, dc = dw.shape
    t = u.shape[0]
    tm = CONV_TILE
    assert seq % tm == 0 and width // 2 <= CONV_HALO and tm % CONV_HALO == 0 and dc % LANES == 0
    nc = dc // LANES
    tiles_per_seq = seq // tm
    hb = tm // CONV_HALO
    n_hblk = t // CONV_HALO
    a_blk, gp_blk = in_col0 // dc, gp_col0 // dc
    wpad = -width % 8
    dw3 = jnp.pad(dw, ((0, wpad), (0, 0))).reshape(width + wpad, nc, LANES).transpose(1, 0, 2)
    prev = lambda i: jnp.maximum(i * hb - 1, 0)
    nxt = lambda i: jnp.minimum((i + 1) * hb, n_hblk - 1)
    vec = lambda x: x.reshape(1, dc)
    return pl.pallas_call(
        functools.partial(_conv_kernel, tiles_per_seq=tiles_per_seq, width=width),
        grid=(t // tm,),
        in_specs=[pl.BlockSpec((tm, dc), lambda i: (i, a_blk)),
                  pl.BlockSpec((tm, dc), lambda i: (i, a_blk + 1)),
                  pl.BlockSpec((CONV_HALO, dc), lambda i: (prev(i), a_blk)),
                  pl.BlockSpec((CONV_HALO, dc), lambda i: (prev(i), a_blk + 1)),
                  pl.BlockSpec((CONV_HALO, dc), lambda i: (nxt(i), a_blk)),
                  pl.BlockSpec((CONV_HALO, dc), lambda i: (nxt(i), a_blk + 1)),
                  pl.BlockSpec((tm, dc), lambda i: (i, gp_blk)),
                  pl.BlockSpec((nc, width + wpad, LANES), lambda i: (0, 0, 0)),
                  pl.BlockSpec((nc, 1, LANES), lambda i: (0, 0, 0)),
                  pl.BlockSpec((1, dc), lambda i: (0, 0)),
                  pl.BlockSpec((1, dc), lambda i: (0, 0)),
                  pl.BlockSpec((dc, dc), lambda i: (0, 0))],
        out_specs=pl.BlockSpec((tm, dc), lambda i: (i, 0)),
        out_shape=jax.ShapeDtypeStruct((t, dc), BF16),
        scratch_shapes=[pltpu.VMEM((nc, tm + 2 * CONV_HALO, LANES), F32),
                        pltpu.VMEM((nc, tm, LANES), F32),
                        pltpu.VMEM((tm, dc), BF16)],
        compiler_params=_cparams("parallel"),
    )(u, u, u, u, u, u, u, dw3, dw_b.reshape(nc, 1, LANES), vec(ln_g), vec(ln_b), w_pw)


def _with_casts(fn, wanted, wts, bf16):
    todo = [key for key in wanted if key not in bf16]
    out, copies = fn(side=[wts[name][jj] for name, jj in todo])
    bf16.update(zip(todo, copies))
    return out


def _in_proj(h, i, wts, bf16):
    j = i // 2
    own = ("w_in_even", j) if i % 2 == 0 else ("w_in_odd", j)
    wanted = [("w_out_even", j)] if i % 2 == 0 else [("w_out_odd", j), ("w_conv_pw", j)]
    if own not in bf16:
        bf16[own] = wts[own[0]][own[1]].astype(BF16)
    return _with_casts(functools.partial(_matmul, h, bf16[own]), wanted, wts, bf16)


def _layer_out(y1, y2, x, mod3, row_of, i, wts, bf16):
    j = i // 2
    own = ("w_out_even", j) if i % 2 == 0 else ("w_out_odd", j)
    wanted = []
    if i + 1 < wts["norm_g"].shape[0]:
        wanted.append(("w_in_odd" if i % 2 == 0 else "w_in_even", (i + 1) // 2))
    return _with_casts(functools.partial(_out_proj, y1, y2, bf16[own], x, mod3, row_of), wanted, wts, bf16)


def _trunk(x, nb, seq, row_of, mod, caches, wts, bf16):
    d = x.shape[1]
    depth = mod.shape[0]
    new_k, new_v = [], []
    for i in range(depth):
        mod3 = mod[i][:, None, :]
        h = _norm_mod(x, wts["norm_g"][i], mod3, row_of)
        j = i // 2
        if i % 2 == 0:
            u = _in_proj(h, i, wts, bf16)
            d_f = wts["w_fourier"].shape[1] * wts["w_fourier"].shape[2]
            d_inner = d_f + wts["w_pool"].shape[1] * wts["w_pool"].shape[2]
            y1 = _fourier(u, wts["dft_w"][j], nb, seq, gp_col0=d_inner)
            y2 = _pool(u, wts["w_pool"][j], wts["pool_scale"][j], nb, seq, in_col0=d_f,
                       gp_col0=d_inner + d_f)
            x = _layer_out(y1, y2, x, mod3, row_of, i, wts, bf16)
        else:
            u = _in_proj(h, i, wts, bf16)
            d_conv = wts["conv_dw"].shape[2]
            d_attn = (u.shape[1] - 3 * d_conv) // 4
            gp_col0 = 3 * d_attn + 2 * d_conv
            if caches is None:
                y1, sk, sv = _ctx_attn(u, wts["q_norm_g"][j], wts["k_norm_g"][j], nb, seq, d_attn, gp_col0)
                new_k.append(sk)
                new_v.append(sv)
            else:
                y1 = _nbr_attn(u, caches[0], caches[1], j, wts["rel_bias"][j], wts["q_norm_g"][j],
                               wts["k_norm_g"][j], nb, seq, d_attn, gp_col0)
            y2 = _conv_module(u, wts["conv_dw"][j], wts["conv_dw_b"][j], wts["conv_ln_g"][j],
                              wts["conv_ln_b"][j], bf16["w_conv_pw", j], seq, in_col0=3 * d_attn,
                              gp_col0=gp_col0 + d_attn)
            x = _layer_out(y1, y2, x, mod3, row_of, i, wts, bf16)
    return x, new_k, new_v


def kernel(x_prompt, x_sample, cache_k, cache_v, c, c_ctx, norm_g, w_ada, b_ada, w_in_even, w_out_even,
           w_fourier, w_pool, pool_scale, w_in_odd, w_out_odd, q_norm_g, k_norm_g, rel_bias, conv_dw,
           conv_dw_b, conv_ln_g, conv_ln_b, w_conv_pw):
    nb_p, seq_p, d = x_prompt.shape
    nb_s, seq_s, _ = x_sample.shape
    n_rows = -(-(1 + nb_s) // 8) * 8
    cond = jnp.concatenate([c_ctx[None, :], c, jnp.zeros((n_rows - 1 - nb_s, d), F32)], axis=0)
    mod = _adaln(cond, w_ada, b_ada)
    wts = dict(norm_g=norm_g, pool_scale=pool_scale, q_norm_g=q_norm_g, k_norm_g=k_norm_g,
               rel_bias=rel_bias, conv_dw=conv_dw, conv_dw_b=conv_dw_b, conv_ln_g=conv_ln_g,
               conv_ln_b=conv_ln_b, w_in_even=w_in_even, w_out_even=w_out_even, w_in_odd=w_in_odd,
               w_out_odd=w_out_odd, w_conv_pw=w_conv_pw, w_fourier=w_fourier,
               w_pool=w_pool.astype(BF16))
    wts["dft_w"] = [_dft_fold(w_fourier[j]) for j in range(w_fourier.shape[0])]
    bf16 = {}
    y_p, ks, vs = _trunk(x_prompt.reshape(nb_p * seq_p, d), nb_p, seq_p, lambda r: 0, mod, None, wts, bf16)
    y_s, _, _ = _trunk(x_sample.reshape(nb_s * seq_s, d), nb_s, seq_s, lambda r: 1 + r // seq_s, mod,
                       (cache_k, cache_v), wts, bf16)
    state_k = jnp.concatenate(ks, axis=1)
    state_v = jnp.concatenate(vs, axis=1)
    return (y_p.reshape(nb_p, seq_p, d), y_s.reshape(nb_s, seq_s, d), state_k, state_v)
```

```python
import functools

import numpy as np
import jax
import jax.numpy as jnp
from jax import lax
from jax.experimental import pallas as pl
from jax.experimental.pallas import tpu as pltpu

F32 = jnp.float32
BF16 = jnp.bfloat16

GRID_W = 64
POOL_WINDOWS = (2, 4, 8, 16)
EPS = 1e-6
NEG_INF = -1e30
LOG2E = float(np.log2(np.e))

LANES = 128
V7X_VMEM_LIMIT_BYTES = 56 * 1024 * 1024

NORM_ROWS = 16

NA_QROWS = 8
NA_KROWS = 16
POOL_TILE = 256
POOL_HALO = 128
POOL_BLOCK_ROWS = 2048
FOURIER_BLOCK_ROWS = 1024
FOURIER_REV_ROWS = 256
FOURIER_FOLD_MIN_SEQ = 1024
CONV_TILE = 256
CONV_HALO = 16
CONV_ROWS = 64


def _cparams(*sem):
    return pltpu.CompilerParams(dimension_semantics=sem, vmem_limit_bytes=V7X_VMEM_LIMIT_BYTES)


def _pick(n, pref, unit=LANES):
    if n <= pref:
        return n
    t = (pref // unit) * unit
    while t > unit and n % t:
        t -= unit
    assert n % t == 0, (n, pref, unit)
    return t


def _sigmoid(x):
    return 0.5 * jnp.tanh(0.5 * x) + 0.5


def _silu(x):
    return x * _sigmoid(x)


def _rms(x, g):
    return x * lax.rsqrt(jnp.mean(x * x, axis=-1, keepdims=True) + EPS) * g


def _adaln_kernel(cond_ref, w_ref, b_ref, o_ref):
    s = _silu(cond_ref[...]).astype(BF16)
    o_ref[0] = jnp.dot(s, w_ref[0].astype(BF16), preferred_element_type=F32) + b_ref[0]


def _adaln(cond, w_ada, b_ada):
    depth, d, n = w_ada.shape
    r = cond.shape[0]
    tn = _pick(n, 1024)
    return pl.pallas_call(
        _adaln_kernel,
        grid=(depth, n // tn),
        in_specs=[pl.BlockSpec((r, d), lambda l, j: (0, 0)),
                  pl.BlockSpec((1, d, tn), lambda l, j: (l, 0, j)),
                  pl.BlockSpec((1, 1, tn), lambda l, j: (l, 0, j))],
        out_specs=pl.BlockSpec((1, r, tn), lambda l, j: (l, 0, j)),
        out_shape=jax.ShapeDtypeStruct((depth, r, n), F32),
        compiler_params=_cparams("parallel", "parallel"),
    )(cond, w_ada, b_ada.reshape(depth, 1, n))


def _norm_mod_kernel(x_ref, g_ref, shift_ref, scale_ref, o_ref, gain_scr, shift_scr):
    gain_scr[...] = jnp.broadcast_to(g_ref[...] * (1.0 + scale_ref[0]), gain_scr.shape)
    shift_scr[...] = jnp.broadcast_to(shift_ref[0], shift_scr.shape)

    def rows16(r, carry):
        rows = pl.ds(pl.multiple_of(r * NORM_ROWS, NORM_ROWS), NORM_ROWS)
        x = x_ref[rows, :]
        rstd = lax.rsqrt(jnp.mean(x * x, axis=-1, keepdims=True) + EPS)
        o_ref[rows, :] = (x * rstd * gain_scr[...] + shift_scr[...]).astype(o_ref.dtype)
        return carry

    lax.fori_loop(0, x_ref.shape[0] // NORM_ROWS, rows16, 0, unroll=4)


def _norm_mod(x, g, mod3, row_of):
    t, d = x.shape
    tm = _pick(t, 512, 8)
    return pl.pallas_call(
        _norm_mod_kernel,
        grid=(t // tm,),
        in_specs=[pl.BlockSpec((tm, d), lambda i: (i, 0)),
                  pl.BlockSpec((1, d), lambda i: (0, 0)),
                  pl.BlockSpec((1, 1, d), lambda i: (row_of(i * tm), 0, 0)),
                  pl.BlockSpec((1, 1, d), lambda i: (row_of(i * tm), 0, 1))],
        out_specs=pl.BlockSpec((tm, d), lambda i: (i, 0)),
        out_shape=jax.ShapeDtypeStruct((t, d), BF16),
        scratch_shapes=[pltpu.VMEM((NORM_ROWS, d), F32), pltpu.VMEM((NORM_ROWS, d), F32)],
        compiler_params=_cparams("parallel"),
    )(x, g.reshape(1, d), mod3, mod3)


def _side_casts(side, gi, gj):
    specs, shapes = [], []
    for s in side:
        rows, cols = s.shape
        nblk = min(gi * gj, rows // 16)
        while rows % nblk or (rows // nblk) % 16:
            nblk -= 1
        specs.append(pl.BlockSpec((rows // nblk, cols),
                                  lambda i, j, nblk=nblk: (jnp.minimum(i * gj + j, nblk - 1), 0)))
        shapes.append(jax.ShapeDtypeStruct(s.shape, BF16))
    return specs, shapes


def _cast_side(side_refs):
    n = len(side_refs) // 2
    for src_ref, dst_ref in zip(side_refs[:n], side_refs[n:]):
        dst_ref[...] = src_ref[...].astype(dst_ref.dtype)


def _matmul_kernel(a_ref, w_ref, *refs):
    n_side = (len(refs) - 1) // 2
    o_ref = refs[n_side]
    o_ref[...] = jnp.dot(a_ref[...], w_ref[...], preferred_element_type=F32).astype(o_ref.dtype)
    _cast_side(refs[:n_side] + refs[n_side + 1:])


def _matmul(a, w, side=()):
    m, k = a.shape
    n = w.shape[1]
    bm, bn = _pick(m, 1024, 8), _pick(n, 1024)
    gi, gj = m // bm, n // bn
    side_specs, side_shapes = _side_casts(side, gi, gj)
    outs = pl.pallas_call(
        _matmul_kernel,
        grid=(gi, gj),
        in_specs=[pl.BlockSpec((bm, k), lambda i, j: (i, 0)),
                  pl.BlockSpec((k, bn), lambda i, j: (0, j))] + side_specs,
        out_specs=[pl.BlockSpec((bm, bn), lambda i, j: (i, j))] + side_specs,
        out_shape=[jax.ShapeDtypeStruct((m, n), BF16)] + side_shapes,
        compiler_params=_cparams("arbitrary", "arbitrary"),
    )(a, w, *side)
    return outs[0], list(outs[1:])


def _out_proj_kernel(a1_ref, a2_ref, w1_ref, w2_ref, x_ref, gate_ref, *refs):
    n_side = (len(refs) - 1) // 2
    o_ref = refs[n_side]
    acc = jnp.dot(a1_ref[...], w1_ref[...], preferred_element_type=F32)
    acc = acc + jnp.dot(a2_ref[...], w2_ref[...], preferred_element_type=F32)
    o_ref[...] = x_ref[...] + gate_ref[0] * acc
    _cast_side(refs[:n_side] + refs[n_side + 1:])


def _out_proj(a1, a2, w, x, mod3, row_of, side=()):
    t, k1 = a1.shape
    d = w.shape[1]
    assert a2.shape == (t, k1) and w.shape[0] == 2 * k1
    bm, bn = _pick(t, 1024, 8), _pick(d, 512 if side else 1024)
    gi, gj = t // bm, d // bn
    gate_blk = 2 * gj
    side_specs, side_shapes = _side_casts(side, gi, gj)
    outs = pl.pallas_call(
        _out_proj_kernel,
        grid=(gi, gj),
        in_specs=[pl.BlockSpec((bm, k1), lambda i, j: (i, 0)),
                  pl.BlockSpec((bm, k1), lambda i, j: (i, 0)),
                  pl.BlockSpec((k1, bn), lambda i, j: (0, j)),
                  pl.BlockSpec((k1, bn), lambda i, j: (1, j)),
                  pl.BlockSpec((bm, bn), lambda i, j: (i, j)),
                  pl.BlockSpec((1, 1, bn), lambda i, j: (row_of(i * bm), 0, gate_blk + j))] + side_specs,
        out_specs=[pl.BlockSpec((bm, bn), lambda i, j: (i, j))] + side_specs,
        out_shape=[jax.ShapeDtypeStruct((t, d), F32)] + side_shapes,
        compiler_params=_cparams("arbitrary", "arbitrary"),
    )(a1, a2, w, w, x, mod3, *side)
    return outs[0], list(outs[1:])


def _dft_cos_sin(n):
    idx = np.arange(n, dtype=np.int64)
    ang = 2.0 * np.pi * ((idx[:, None] * idx[None, :]) % n) / n
    return np.cos(ang) / np.sqrt(n), np.sin(ang) / np.sqrt(n)


def _dft_fold_kernel(c_ref, s_ref, wf_ref, o_ref):
    cg = wf_ref.shape[1]
    hi = lax.Precision.HIGHEST
    o_ref[0, :, 0:cg] = jnp.dot(c_ref[...], wf_ref[0], precision=hi, preferred_element_type=F32).astype(BF16)
    o_ref[0, :, cg:] = jnp.dot(s_ref[...], wf_ref[0], precision=hi, preferred_element_type=F32).astype(BF16)


def _dft_fold(wf):
    groups, cg, _ = wf.shape
    cc, sc = _dft_cos_sin(cg)
    const = pl.BlockSpec((cg, cg), lambda g: (0, 0))
    return pl.pallas_call(
        _dft_fold_kernel,
        grid=(groups,),
        in_specs=[const, const, pl.BlockSpec((1, cg, cg), lambda g: (g, 0, 0))],
        out_specs=pl.BlockSpec((1, cg, 2 * cg), lambda g: (g, 0, 0)),
        out_shape=jax.ShapeDtypeStruct((groups, cg, 2 * cg), BF16),
        compiler_params=_cparams("parallel"),
    )(jnp.asarray(cc, F32), jnp.asarray(sc, F32), wf)


def _fourier_kernel(u_ref, gp_ref, cw_ref, csl_ref, jsh_ref, o_ref, q_scr, mid_scr, *, seq, cg, fold):
    if not fold:
        @pl.when(pl.program_id(2) == 0)
        def _():
            for s in range(u_ref.shape[0] // seq):
                p = jnp.dot(u_ref[s * seq:(s + 1) * seq, :], cw_ref[0], preferred_element_type=F32)
                q_scr[s, 0:seq, :] = p[:, :cg].astype(BF16)
                q_scr[s, seq:2 * seq, :] = p[:, cg:].astype(BF16)

        for s in range(u_ref.shape[0] // seq):
            rows = slice(s * csl_ref.shape[0], (s + 1) * csl_ref.shape[0])
            y = jnp.dot(csl_ref[...], q_scr[s], preferred_element_type=F32)
            o_ref[rows, :] = (y * _silu(gp_ref[rows, :].astype(F32))).astype(o_ref.dtype)
        return

    half = seq // 2
    rb = jsh_ref.shape[0]
    nblk = half // rb
    n_seq = u_ref.shape[0] // seq
    tr = csl_ref.shape[0]

    @pl.when(pl.program_id(2) == 0)
    def _():
        row0 = lax.broadcasted_iota(jnp.int32, (rb, 2 * cg), 0) == 0
        sign = jnp.where(lax.broadcasted_iota(jnp.int32, (1, 2 * cg), 1) < cg, 1.0, -1.0)
        for s in range(n_seq):
            p = jnp.dot(u_ref[s * seq:(s + 1) * seq, :], cw_ref[0], preferred_element_type=F32)
            mid_scr[s] = p[half:half + 8, :cg]
            upper = p[half:, :].astype(BF16)
            for a in range(nblk):
                rev = jnp.dot(jsh_ref[...], upper[(nblk - 1 - a) * rb:(nblk - a) * rb, :],
                              preferred_element_type=F32)
                if a > 0:
                    first = half + (nblk - a) * rb
                    rev = jnp.where(row0, p[first:first + 1, :], rev)
                q = p[a * rb:(a + 1) * rb, :] + sign * rev
                q_scr[s, a * rb:(a + 1) * rb, :] = q[:, :cg].astype(BF16)
                q_scr[s, half + a * rb:half + (a + 1) * rb, :] = q[:, cg:].astype(BF16)

    odd_row = (lax.broadcasted_iota(jnp.int32, (tr, 1), 0) & 1) == 1
    cos_mid = jnp.where(odd_row, -1.0, 1.0) * seq ** -0.5
    for s in range(n_seq):
        rows = slice(s * tr, (s + 1) * tr)
        y = jnp.dot(csl_ref[...], q_scr[s], preferred_element_type=F32) + cos_mid * mid_scr[s, 0:1, :]
        o_ref[rows, :] = (y * _silu(gp_ref[rows, :].astype(F32))).astype(o_ref.dtype)


def _fourier(u, cw, nb, seq, gp_col0):
    groups, cg, _ = cw.shape
    t = u.shape[0]
    tr = _pick(seq, 512, 8)
    rt = seq // tr
    spb = 1
    if rt == 1:
        spb = max(1, min(nb, FOURIER_BLOCK_ROWS // seq))
        while nb % spb:
            spb -= 1
    fold = seq >= FOURIER_FOLD_MIN_SEQ
    half = seq // 2
    assert tr % 2 == 0 and half % 16 == 0
    cl, sl = _dft_cos_sin(seq)
    kpos = half if fold else seq
    csl = jnp.asarray(np.concatenate([cl[:, :kpos], -sl[:, :kpos]], axis=1), BF16)
    rb = _pick(half, FOURIER_REV_ROWS, 16)
    jsh = np.zeros((rb, rb), np.float32)
    jsh[np.arange(1, rb), rb - np.arange(1, rb)] = 1.0
    gp_blk0 = gp_col0 // cg
    return pl.pallas_call(
        functools.partial(_fourier_kernel, seq=seq, cg=cg, fold=fold),
        grid=(nb // spb, groups, rt),
        in_specs=[pl.BlockSpec((spb * seq, cg), lambda b, g, r: (b, g)),
                  pl.BlockSpec((spb * tr, cg), lambda b, g, r: (b * rt + r, gp_blk0 + g)),
                  pl.BlockSpec((1, cg, 2 * cg), lambda b, g, r: (g, 0, 0)),
                  pl.BlockSpec((tr, 2 * kpos), lambda b, g, r: (r, 0)),
                  pl.BlockSpec((rb, rb), lambda b, g, r: (0, 0))],
        out_specs=pl.BlockSpec((spb * tr, cg), lambda b, g, r: (b * rt + r, g)),
        out_shape=jax.ShapeDtypeStruct((t, groups * cg), BF16),
        scratch_shapes=[pltpu.VMEM((spb, 2 * kpos, cg), BF16), pltpu.VMEM((spb, 8, cg), F32)],
        compiler_params=_cparams("parallel", "parallel", "arbitrary"),
    )(u, u, cw, csl, jnp.asarray(jsh, BF16))


def _pool_tables(seq):
    r = np.arange(POOL_TILE)[:, None]
    s = np.arange(POOL_TILE + 2 * POOL_HALO)[None, :] - POOL_HALO
    tpos = np.arange(seq)
    bands, cnts = [], []
    for win in POOL_WINDOWS:
        half = win // 2
        assert half <= POOL_HALO
        bands.append(((s >= r - half) & (s <= r + half - 1)).astype(np.float32))
        cnts.append(np.minimum(tpos + half, seq) - np.maximum(tpos - half, 0))
    return np.stack(bands), np.stack(cnts).astype(np.float32)[:, :, None]


def _pool_kernel(u_ref, gp_ref, band_ref, cnt_ref, w_ref, ps_ref, o_ref, *, seq):
    tr = POOL_TILE
    for s0 in range(0, u_ref.shape[0], seq):
        for t in range(seq // tr):
            base = tr * t - POOL_HALO
            lo, hi = max(base, 0), min(base + tr + 2 * POOL_HALO, seq)
            rows = slice(s0 + tr * t, s0 + tr * (t + 1))
            wsum = jnp.dot(band_ref[0, :, lo - base:hi - base], u_ref[s0 + lo:s0 + hi, :],
                           preferred_element_type=F32)
            cnt = cnt_ref[0, tr * t:tr * (t + 1), :]
            d = (wsum / cnt - u_ref[rows, :].astype(F32)).astype(BF16)
            y = jnp.dot(d, w_ref[0], preferred_element_type=F32) * ps_ref[...]
            o_ref[rows, :] = (y * _silu(gp_ref[rows, :].astype(F32))).astype(o_ref.dtype)


def _pool(u, wp, pool_scale, nb, seq, in_col0, gp_col0):
    groups, cg, _ = wp.shape
    assert groups == len(POOL_WINDOWS) and seq % POOL_TILE == 0
    t = u.shape[0]
    band, cnt = _pool_tables(seq)
    in_blk0, gp_blk0 = in_col0 // cg, gp_col0 // cg
    spb = max(1, min(nb, POOL_BLOCK_ROWS // seq))
    while nb % spb:
        spb -= 1
    rows = spb * seq
    return pl.pallas_call(
        functools.partial(_pool_kernel, seq=seq),
        grid=(nb // spb, groups),
        in_specs=[pl.BlockSpec((rows, cg), lambda b, g: (b, in_blk0 + g)),
                  pl.BlockSpec((rows, cg), lambda b, g: (b, gp_blk0 + g)),
                  pl.BlockSpec((1,) + band.shape[1:], lambda b, g: (g, 0, 0)),
                  pl.BlockSpec((1, seq, 1), lambda b, g: (g, 0, 0)),
                  pl.BlockSpec((1, cg, cg), lambda b, g: (g, 0, 0)),
                  pl.BlockSpec((1, cg), lambda b, g: (0, g))],
        out_specs=pl.BlockSpec((rows, cg), lambda b, g: (b, g)),
        out_shape=jax.ShapeDtypeStruct((t, groups * cg), BF16),
        compiler_params=_cparams("parallel", "parallel"),
    )(u, u, jnp.asarray(band, BF16), jnp.asarray(cnt, F32), wp, pool_scale.reshape(1, groups * cg))


def _ctx_attn_kernel(q_ref, k_ref, v_ref, gp_ref, qg_ref, kg_ref, o_ref, sk_ref, sv_ref, *, heads, hd):
    qscale = hd ** -0.5 * LOG2E
    for h in range(heads):
        cols = slice(h * hd, (h + 1) * hd)
        qn = _rms(q_ref[:, cols].astype(F32), qg_ref[...]) * qscale
        kn = _rms(k_ref[:, cols].astype(F32), kg_ref[...])
        v = v_ref[:, cols]
        sk_ref[0, 0, h] = kn
        sv_ref[0, 0, h] = v.astype(F32)
        s = lax.dot_general(qn.astype(BF16), kn.astype(BF16), (((1,), (1,)), ((), ())),
                            preferred_element_type=F32)
        p = jnp.exp2(s - jnp.max(s, axis=-1, keepdims=True))
        v_ones = jnp.concatenate([v, jnp.ones_like(v)], axis=1)
        o = jnp.dot(p.astype(BF16), v_ones, preferred_element_type=F32)
        o = o[:, :hd] / o[:, hd:]
        o_ref[:, cols] = (o * _silu(gp_ref[:, cols].astype(F32))).astype(o_ref.dtype)


def _ctx_attn(u, qg, kg, nb, seq, d_attn, gp_col0):
    hd = qg.shape[-1]
    heads = d_attn // hd
    t = u.shape[0]
    state = jax.ShapeDtypeStruct((nb, 1, heads, seq, hd), F32)
    state_spec = pl.BlockSpec((1, 1, heads, seq, hd), lambda b: (b, 0, 0, 0, 0))
    gp_blk = gp_col0 // d_attn
    return pl.pallas_call(
        functools.partial(_ctx_attn_kernel, heads=heads, hd=hd),
        grid=(nb,),
        in_specs=[pl.BlockSpec((seq, d_attn), lambda b: (b, 0)),
                  pl.BlockSpec((seq, d_attn), lambda b: (b, 1)),
                  pl.BlockSpec((seq, d_attn), lambda b: (b, 2)),
                  pl.BlockSpec((seq, d_attn), lambda b: (b, gp_blk)),
                  pl.BlockSpec((1, hd), lambda b: (0, 0)),
                  pl.BlockSpec((1, hd), lambda b: (0, 0))],
        out_specs=[pl.BlockSpec((seq, d_attn), lambda b: (b, 0)), state_spec, state_spec],
        out_shape=[jax.ShapeDtypeStruct((t, d_attn), BF16), state, state],
        compiler_params=_cparams("parallel"),
    )(u, u, u, u, qg.reshape(1, hd), kg.reshape(1, hd))


def _nbr_plan(rows, kh, n_kh):
    assert rows % NA_QROWS == 0 and rows >= NA_KROWS
    n_dr = 2 * n_kh - 1
    blocks, types = [], []
    for r0 in range(0, rows, NA_QROWS):
        ws = int(np.clip(r0 - kh // 2, 0, rows - NA_KROWS))
        idx = np.full((NA_QROWS, NA_KROWS), n_dr, np.int32)
        for i in range(NA_QROWS):
            r = r0 + i
            start = int(np.clip(r - kh // 2, 0, rows - kh))
            assert ws <= start and start + kh <= ws + NA_KROWS
            for j in range(NA_KROWS):
                rk = ws + j
                if start <= rk < start + kh:
                    idx[i, j] = rk - r + n_kh - 1
        for t, known in enumerate(types):
            if np.array_equal(known, idx):
                break
        else:
            t = len(types)
            types.append(idx)
        blocks.append((r0, ws, t))
    return tuple(blocks), np.stack(types)


def _nbr_build_bias(rb_ref, bias_scr, head, blk_idx, n_dr, n_dc, kw):
    w = GRID_W
    c = lax.broadcasted_iota(jnp.int32, (w, 2 * w), 0)
    lane = lax.broadcasted_iota(jnp.int32, (w, 2 * w), 1)
    cp = lane % w
    q_start = jnp.clip(c - kw // 2, 0, w - kw)
    diff = jnp.where(cp >= q_start, jnp.where(cp < q_start + kw, cp - c + (kw - 1), -1), -1)
    neg = jnp.full((w, 2 * w), NEG_INF, F32)
    pairs = []
    for dr in range(n_dr):
        blk = neg
        for d in range(n_dc):
            blk = jnp.where(diff == d, rb_ref[(head * n_dr + dr) * n_dc + d] * LOG2E, blk)
        pairs.append(blk)
    pairs.append(neg)
    low = lane < w
    nt, qr, kr = blk_idx.shape
    for t in range(nt):
        for i in range(qr):
            for j in range(0, kr, 2):
                e, o = int(blk_idx[t, i, j]), int(blk_idx[t, i, j + 1])
                blk = pairs[e] if e == o else jnp.where(low, pairs[e], pairs[o])
                bias_scr[t, i * w:(i + 1) * w, j * w:(j + 2) * w] = blk


def _nbr_attn_kernel(rb_ref, q_ref, k_ref, v_ref, ck_ref, cv_ref, gp_ref, qg_ref, kg_ref, o_ref,
                     qn_scr, kn_scr, v1_scr, bias_scr, *, blocks, blk_idx, hd, kw, n_dr, n_dc):
    @pl.when(pl.program_id(1) == 0)
    def _():
        _nbr_build_bias(rb_ref, bias_scr, pl.program_id(0), blk_idx, n_dr, n_dc, kw)

    qscale = hd ** -0.5 * LOG2E
    qb, kb = NA_QROWS * GRID_W, NA_KROWS * GRID_W
    nt_dims = (((1,), (1,)), ((), ()))
    qn_scr[...] = (_rms(q_ref[...].astype(F32), qg_ref[...]) * qscale).astype(BF16)
    kn_scr[...] = _rms(k_ref[...].astype(F32), kg_ref[...]).astype(BF16)
    ck = ck_ref[0, 0, 0].astype(BF16)
    cv = cv_ref[0, 0, 0].astype(BF16)
    cv1 = jnp.concatenate([cv, jnp.ones_like(cv)], axis=1)
    v1_scr[:, 0:hd] = v_ref[...]
    v1_scr[:, hd:] = jnp.ones(v_ref.shape, BF16)

    def logits(n):
        r0, ws, t = blocks[n]
        q = qn_scr[r0 * GRID_W:r0 * GRID_W + qb, :]
        s_loc = lax.dot_general(q, kn_scr[ws * GRID_W:ws * GRID_W + kb, :], nt_dims,
                                preferred_element_type=F32) + bias_scr[t]
        return s_loc, lax.dot_general(q, ck, nt_dims, preferred_element_type=F32)

    def softmax(s_loc, s_ctx):
        m = jnp.maximum(jnp.max(s_loc, axis=-1, keepdims=True), jnp.max(s_ctx, axis=-1, keepdims=True))
        p_loc = jnp.exp2(s_loc - m)
        p_ctx = jnp.exp2(s_ctx - m)
        return p_loc.astype(BF16), p_ctx.astype(BF16)

    def attend(n, p_loc, p_ctx):
        r0, ws, _ = blocks[n]
        qs = slice(r0 * GRID_W, r0 * GRID_W + qb)
        o = jnp.dot(p_loc, v1_scr[ws * GRID_W:ws * GRID_W + kb, :], preferred_element_type=F32)
        o = o + jnp.dot(p_ctx, cv1, preferred_element_type=F32)
        o = o[:, :hd] / o[:, hd:]
        o_ref[qs, :] = (o * _silu(gp_ref[qs, :].astype(F32))).astype(o_ref.dtype)

    s_cur, p_prev = logits(0), None
    for n in range(len(blocks)):
        p_cur = softmax(*s_cur)
        if p_prev is not None:
            attend(n - 1, *p_prev)
        if n + 1 < len(blocks):
            s_cur = logits(n + 1)
        p_prev = p_cur
    attend(len(blocks) - 1, *p_prev)


def _nbr_attn(u, cache_k, cache_v, layer, rel_bias, qg, kg, nb, seq, d_attn, gp_col0):
    hd = qg.shape[-1]
    heads = d_attn // hd
    t = u.shape[0]
    past = cache_k.shape[3]
    rows = seq // GRID_W
    _, n_dr, n_dc = rel_bias.shape
    n_kh, kw = (n_dr + 1) // 2, (n_dc + 1) // 2
    assert 2 * GRID_W == LANES and kw <= GRID_W
    blocks, blk_idx = _nbr_plan(rows, min(n_kh, rows), n_kh)
    gp_blk0 = gp_col0 // hd
    cache_spec = pl.BlockSpec((1, 1, 1, past, hd), lambda h, b: (b, layer, h, 0, 0))
    return pl.pallas_call(
        functools.partial(_nbr_attn_kernel, blocks=blocks, blk_idx=blk_idx, hd=hd, kw=kw, n_dr=n_dr,
                          n_dc=n_dc),
        grid=(heads, nb),
        in_specs=[pl.BlockSpec(memory_space=pltpu.SMEM),
                  pl.BlockSpec((seq, hd), lambda h, b: (b, h)),
                  pl.BlockSpec((seq, hd), lambda h, b: (b, heads + h)),
                  pl.BlockSpec((seq, hd), lambda h, b: (b, 2 * heads + h)),
                  cache_spec, cache_spec,
                  pl.BlockSpec((seq, hd), lambda h, b: (b, gp_blk0 + h)),
                  pl.BlockSpec((1, hd), lambda h, b: (0, 0)),
                  pl.BlockSpec((1, hd), lambda h, b: (0, 0))],
        out_specs=pl.BlockSpec((seq, hd), lambda h, b: (b, h)),
        out_shape=jax.ShapeDtypeStruct((t, d_attn), BF16),
        scratch_shapes=[pltpu.VMEM((seq, hd), BF16), pltpu.VMEM((seq, hd), BF16),
                        pltpu.VMEM((seq, 2 * hd), BF16),
                        pltpu.VMEM((blk_idx.shape[0], NA_QROWS * GRID_W, NA_KROWS * GRID_W), F32)],
        compiler_params=_cparams("arbitrary", "arbitrary"),
    )(rel_bias.reshape(-1), u, u, u, cache_k, cache_v, u, qg.reshape(1, hd), kg.reshape(1, hd))


def _conv_kernel(a_ref, b_ref, ap_ref, bp_ref, an_ref, bn_ref, gp_ref, dw_ref, dwb_ref, lng_ref, lnb_ref,
                 wpw_ref, o_ref, h_scr, c_scr, z_scr, *, tiles_per_seq, width):
    tm, dc = a_ref.shape
    nc = dc // LANES
    pos = pl.program_id(0) % tiles_per_seq
    keep_prev = (pos > 0).astype(F32)
    keep_next = (pos < tiles_per_seq - 1).astype(F32)
    for c in range(nc):
        cols = slice(c * LANES, (c + 1) * LANES)
        h_scr[c, 0:CONV_HALO, :] = keep_prev * (
            ap_ref[:, cols].astype(F32) * _sigmoid(bp_ref[:, cols].astype(F32)))
        h_scr[c, CONV_HALO:CONV_HALO + tm, :] = (
            a_ref[:, cols].astype(F32) * _sigmoid(b_ref[:, cols].astype(F32)))
        h_scr[c, CONV_HALO + tm:, :] = keep_next * (
            an_ref[:, cols].astype(F32) * _sigmoid(bn_ref[:, cols].astype(F32)))

    off = CONV_HALO - width // 2

    def chunk(c, carry):
        for rb in range(tm // CONV_ROWS):
            acc = jnp.zeros((CONV_ROWS, LANES), F32)
            for k in range(width):
                acc = acc + dw_ref[c, k:k + 1, :] * h_scr[c, pl.ds(rb * CONV_ROWS + off + k, CONV_ROWS), :]
            c_scr[c, rb * CONV_ROWS:(rb + 1) * CONV_ROWS, :] = acc + dwb_ref[c]
        return carry

    lax.fori_loop(0, nc, chunk, 0)

    s1 = jnp.zeros((tm, LANES), F32)
    for c in range(nc):
        s1 = s1 + c_scr[c]
    mu = jnp.sum(s1, axis=-1, keepdims=True) / dc
    s2 = jnp.zeros((tm, LANES), F32)
    for c in range(nc):
        xc = c_scr[c] - mu
        s2 = s2 + xc * xc
    rstd = lax.rsqrt(jnp.sum(s2, axis=-1, keepdims=True) / dc + EPS)
    for c in range(nc):
        cols = slice(c * LANES, (c + 1) * LANES)
        y = (c_scr[c] - mu) * rstd * lng_ref[:, cols] + lnb_ref[:, cols]
        z_scr[:, cols] = _silu(y).astype(BF16)
    out = jnp.dot(z_scr[...], wpw_ref[...], preferred_element_type=F32)
    o_ref[...] = (out * _silu(gp_ref[...].astype(F32))).astype(o_ref.dtype)


def _conv_module(u, dw, dw_b, ln_g, ln_b, w_pw, seq, in_col0, gp_col0):
    width, dc = dw.shape
    t = u.shape[0]
    tm = CONV_TILE
    assert seq % tm == 0 and width // 2 <= CONV_HALO and tm % CONV_HALO == 0 and dc % LANES == 0
    nc = dc // LANES
    tiles_per_seq = seq // tm
    hb = tm // CONV_HALO
    n_hblk = t // CONV_HALO
    a_blk, gp_blk = in_col0 // dc, gp_col0 // dc
    wpad = -width % 8
    dw3 = jnp.pad(dw, ((0, wpad), (0, 0))).reshape(width + wpad, nc, LANES).transpose(1, 0, 2)
    prev = lambda i: jnp.maximum(i * hb - 1, 0)
    nxt = lambda i: jnp.minimum((i + 1) * hb, n_hblk - 1)
    vec = lambda x: x.reshape(1, dc)
    return pl.pallas_call(
        functools.partial(_conv_kernel, tiles_per_seq=tiles_per_seq, width=width),
        grid=(t // tm,),
        in_specs=[pl.BlockSpec((tm, dc), lambda i: (i, a_blk)),
                  pl.BlockSpec((tm, dc), lambda i: (i, a_blk + 1)),
                  pl.BlockSpec((CONV_HALO, dc), lambda i: (prev(i), a_blk)),
                  pl.BlockSpec((CONV_HALO, dc), lambda i: (prev(i), a_blk + 1)),
                  pl.BlockSpec((CONV_HALO, dc), lambda i: (nxt(i), a_blk)),
                  pl.BlockSpec((CONV_HALO, dc), lambda i: (nxt(i), a_blk + 1)),
                  pl.BlockSpec((tm, dc), lambda i: (i, gp_blk)),
                  pl.BlockSpec((nc, width + wpad, LANES), lambda i: (0, 0, 0)),
                  pl.BlockSpec((nc, 1, LANES), lambda i: (0, 0, 0)),
                  pl.BlockSpec((1, dc), lambda i: (0, 0)),
                  pl.BlockSpec((1, dc), lambda i: (0, 0)),
                  pl.BlockSpec((dc, dc), lambda i: (0, 0))],
        out_specs=pl.BlockSpec((tm, dc), lambda i: (i, 0)),
        out_shape=jax.ShapeDtypeStruct((t, dc), BF16),
        scratch_shapes=[pltpu.VMEM((nc, tm + 2 * CONV_HALO, LANES), F32),
                        pltpu.VMEM((nc, tm, LANES), F32),
                        pltpu.VMEM((tm, dc), BF16)],
        compiler_params=_cparams("parallel"),
    )(u, u, u, u, u, u, u, dw3, dw_b.reshape(nc, 1, LANES), vec(ln_g), vec(ln_b), w_pw)


def _with_casts(fn, wanted, wts, bf16):
    todo = [key for key in wanted if key not in bf16]
    out, copies = fn(side=[wts[name][jj] for name, jj in todo])
    bf16.update(zip(todo, copies))
    return out


def _in_proj(h, i, wts, bf16):
    j = i // 2
    own = ("w_in_even", j) if i % 2 == 0 else ("w_in_odd", j)
    wanted = [("w_out_even", j)] if i % 2 == 0 else [("w_out_odd", j), ("w_conv_pw", j)]
    if own not in bf16:
        bf16[own] = wts[own[0]][own[1]].astype(BF16)
    return _with_casts(functools.partial(_matmul, h, bf16[own]), wanted, wts, bf16)


def _layer_out(y1, y2, x, mod3, row_of, i, wts, bf16):
    j = i // 2
    own = ("w_out_even", j) if i % 2 == 0 else ("w_out_odd", j)
    wanted = []
    if i + 1 < wts["norm_g"].shape[0]:
        wanted.append(("w_in_odd" if i % 2 == 0 else "w_in_even", (i + 1) // 2))
    return _with_casts(functools.partial(_out_proj, y1, y2, bf16[own], x, mod3, row_of), wanted, wts, bf16)


def _trunk(x, nb, seq, row_of, mod, caches, wts, bf16):
    d = x.shape[1]
    depth = mod.shape[0]
    new_k, new_v = [], []
    for i in range(depth):
        mod3 = mod[i][:, None, :]
        h = _norm_mod(x, wts["norm_g"][i], mod3, row_of)
        j = i // 2
        if i % 2 == 0:
            u = _in_proj(h, i, wts, bf16)
            d_f = wts["w_fourier"].shape[1] * wts["w_fourier"].shape[2]
            d_inner = d_f + wts["w_pool"].shape[1] * wts["w_pool"].shape[2]
            y1 = _fourier(u, wts["dft_w"][j], nb, seq, gp_col0=d_inner)
            y2 = _pool(u, wts["w_pool"][j], wts["pool_scale"][j], nb, seq, in_col0=d_f,
                       gp_col0=d_inner + d_f)
            x = _layer_out(y1, y2, x, mod3, row_of, i, wts, bf16)
        else:
            u = _in_proj(h, i, wts, bf16)
            d_conv = wts["conv_dw"].shape[2]
            d_attn = (u.shape[1] - 3 * d_conv) // 4
            gp_col0 = 3 * d_attn + 2 * d_conv
            if caches is None:
                y1, sk, sv = _ctx_attn(u, wts["q_norm_g"][j], wts["k_norm_g"][j], nb, seq, d_attn, gp_col0)
                new_k.append(sk)
                new_v.append(sv)
            else:
                y1 = _nbr_attn(u, caches[0], caches[1], j, wts["rel_bias"][j], wts["q_norm_g"][j],
                               wts["k_norm_g"][j], nb, seq, d_attn, gp_col0)
            y2 = _conv_module(u, wts["conv_dw"][j], wts["conv_dw_b"][j], wts["conv_ln_g"][j],
                              wts["conv_ln_b"][j], bf16["w_conv_pw", j], seq, in_col0=3 * d_attn,
                              gp_col0=gp_col0 + d_attn)
            x = _layer_out(y1, y2, x, mod3, row_of, i, wts, bf16)
    return x, new_k, new_v


def kernel(x_prompt, x_sample, cache_k, cache_v, c, c_ctx, norm_g, w_ada, b_ada, w_in_even, w_out_even,
           w_fourier, w_pool, pool_scale, w_in_odd, w_out_odd, q_norm_g, k_norm_g, rel_bias, conv_dw,
           conv_dw_b, conv_ln_g, conv_ln_b, w_conv_pw):
    nb_p, seq_p, d = x_prompt.shape
    nb_s, seq_s, _ = x_sample.shape
    n_rows = -(-(1 + nb_s) // 8) * 8
    cond = jnp.concatenate([c_ctx[None, :], c, jnp.zeros((n_rows - 1 - nb_s, d), F32)], axis=0)
    mod = _adaln(cond, w_ada, b_ada)
    wts = dict(norm_g=norm_g, pool_scale=pool_scale, q_norm_g=q_norm_g, k_norm_g=k_norm_g,
               rel_bias=rel_bias, conv_dw=conv_dw, conv_dw_b=conv_dw_b, conv_ln_g=conv_ln_g,
               conv_ln_b=conv_ln_b, w_in_even=w_in_even, w_out_even=w_out_even, w_in_odd=w_in_odd,
               w_out_odd=w_out_odd, w_conv_pw=w_conv_pw, w_fourier=w_fourier,
               w_pool=w_pool.astype(BF16))
    wts["dft_w"] = [_dft_fold(w_fourier[j]) for j in range(w_fourier.shape[0])]
    bf16 = {}
    y_p, ks, vs = _trunk(x_prompt.reshape(nb_p * seq_p, d), nb_p, seq_p, lambda r: 0, mod, None, wts, bf16)
    y_s, _, _ = _trunk(x_sample.reshape(nb_s * seq_s, d), nb_s, seq_s, lambda r: 1 + r // seq_s, mod,
                       (cache_k, cache_v), wts, bf16)
    state_k = jnp.concatenate(ks, axis=1)
    state_v = jnp.concatenate(vs, axis=1)
    return (y_p.reshape(nb_p, seq_p, d), y_s.reshape(nb_s, seq_s, d), state_k, state_v)
```

```python
import functools

import numpy as np
import jax
import jax.numpy as jnp
from jax import lax
from jax.experimental import pallas as pl
from jax.experimental.pallas import tpu as pltpu

F32 = jnp.float32
BF16 = jnp.bfloat16

GRID_W = 64
POOL_WINDOWS = (2, 4, 8, 16)
EPS = 1e-6
NEG_INF = -1e30
LOG2E = float(np.log2(np.e))

LANES = 128
V7X_VMEM_LIMIT_BYTES = 60 * 1024 * 1024

NORM_ROWS = 16

NA_QROWS = 4
NA_KROWS = 12
POOL_TILE = 256
POOL_HALO = 128
POOL_BLOCK_ROWS = 2048
FOURIER_BLOCK_ROWS = 1024
FOURIER_REV_ROWS = 256
FOURIER_FOLD_MIN_SEQ = 1024
CONV_TILE = 256
CONV_HALO = 16
CONV_ROWS = 64


def _cparams(*sem):
    return pltpu.CompilerParams(dimension_semantics=sem, vmem_limit_bytes=V7X_VMEM_LIMIT_BYTES)


def _pick(n, pref, unit=LANES):
    if n <= pref:
        return n
    t = (pref // unit) * unit
    while t > unit and n % t:
        t -= unit
    assert n % t == 0, (n, pref, unit)
    return t


def _sigmoid(x):
    return 0.5 * jnp.tanh(0.5 * x) + 0.5


def _silu(x):
    return x * _sigmoid(x)


def _rms(x, g):
    return x * lax.rsqrt(jnp.mean(x * x, axis=-1, keepdims=True) + EPS) * g


def _adaln_kernel(cond_ref, w_ref, b_ref, o_ref):
    s = _silu(cond_ref[...]).astype(BF16)
    o_ref[0] = jnp.dot(s, w_ref[0].astype(BF16), preferred_element_type=F32) + b_ref[0]


def _adaln(cond, w_ada, b_ada):
    depth, d, n = w_ada.shape
    r = cond.shape[0]
    tn = _pick(n, 1024)
    return pl.pallas_call(
        _adaln_kernel,
        grid=(depth, n // tn),
        in_specs=[pl.BlockSpec((r, d), lambda l, j: (0, 0)),
                  pl.BlockSpec((1, d, tn), lambda l, j: (l, 0, j)),
                  pl.BlockSpec((1, 1, tn), lambda l, j: (l, 0, j))],
        out_specs=pl.BlockSpec((1, r, tn), lambda l, j: (l, 0, j)),
        out_shape=jax.ShapeDtypeStruct((depth, r, n), F32),
        compiler_params=_cparams("parallel", "parallel"),
    )(cond, w_ada, b_ada.reshape(depth, 1, n))


def _norm_mod_kernel(x_ref, g_ref, shift_ref, scale_ref, o_ref, gain_scr, shift_scr):
    gain_scr[...] = jnp.broadcast_to(g_ref[...] * (1.0 + scale_ref[0]), gain_scr.shape)
    shift_scr[...] = jnp.broadcast_to(shift_ref[0], shift_scr.shape)

    def rows16(r, carry):
        rows = pl.ds(pl.multiple_of(r * NORM_ROWS, NORM_ROWS), NORM_ROWS)
        x = x_ref[rows, :]
        rstd = lax.rsqrt(jnp.mean(x * x, axis=-1, keepdims=True) + EPS)
        o_ref[rows, :] = (x * rstd * gain_scr[...] + shift_scr[...]).astype(o_ref.dtype)
        return carry

    lax.fori_loop(0, x_ref.shape[0] // NORM_ROWS, rows16, 0, unroll=4)


def _norm_mod(x, g, mod3, row_of):
    t, d = x.shape
    tm = _pick(t, 512, 8)
    return pl.pallas_call(
        _norm_mod_kernel,
        grid=(t // tm,),
        in_specs=[pl.BlockSpec((tm, d), lambda i: (i, 0)),
                  pl.BlockSpec((1, d), lambda i: (0, 0)),
                  pl.BlockSpec((1, 1, d), lambda i: (row_of(i * tm), 0, 0)),
                  pl.BlockSpec((1, 1, d), lambda i: (row_of(i * tm), 0, 1))],
        out_specs=pl.BlockSpec((tm, d), lambda i: (i, 0)),
        out_shape=jax.ShapeDtypeStruct((t, d), BF16),
        scratch_shapes=[pltpu.VMEM((NORM_ROWS, d), F32), pltpu.VMEM((NORM_ROWS, d), F32)],
        compiler_params=_cparams("parallel"),
    )(x, g.reshape(1, d), mod3, mod3)


def _side_casts(side, gi, gj):
    specs, shapes = [], []
    for s in side:
        rows, cols = s.shape
        nblk = min(gi * gj, rows // 16)
        while rows % nblk or (rows // nblk) % 16:
            nblk -= 1
        specs.append(pl.BlockSpec((rows // nblk, cols),
                                  lambda i, j, nblk=nblk: (jnp.minimum(i * gj + j, nblk - 1), 0)))
        shapes.append(jax.ShapeDtypeStruct(s.shape, BF16))
    return specs, shapes


def _cast_side(side_refs):
    n = len(side_refs) // 2
    for src_ref, dst_ref in zip(side_refs[:n], side_refs[n:]):
        dst_ref[...] = src_ref[...].astype(dst_ref.dtype)


def _matmul_kernel(a_ref, w_ref, *refs):
    n_side = (len(refs) - 1) // 2
    o_ref = refs[n_side]
    o_ref[...] = jnp.dot(a_ref[...], w_ref[...], preferred_element_type=F32).astype(o_ref.dtype)
    _cast_side(refs[:n_side] + refs[n_side + 1:])


def _matmul(a, w, side=()):
    m, k = a.shape
    n = w.shape[1]
    bm, bn = _pick(m, 1024, 8), _pick(n, 1024)
    gi, gj = m // bm, n // bn
    side_specs, side_shapes = _side_casts(side, gi, gj)
    outs = pl.pallas_call(
        _matmul_kernel,
        grid=(gi, gj),
        in_specs=[pl.BlockSpec((bm, k), lambda i, j: (i, 0)),
                  pl.BlockSpec((k, bn), lambda i, j: (0, j))] + side_specs,
        out_specs=[pl.BlockSpec((bm, bn), lambda i, j: (i, j))] + side_specs,
        out_shape=[jax.ShapeDtypeStruct((m, n), BF16)] + side_shapes,
        compiler_params=_cparams("arbitrary", "arbitrary"),
    )(a, w, *side)
    return outs[0], list(outs[1:])


def _out_proj_kernel(a1_ref, a2_ref, w1_ref, w2_ref, x_ref, gate_ref, *refs):
    n_side = (len(refs) - 1) // 2
    o_ref = refs[n_side]
    acc = jnp.dot(a1_ref[...], w1_ref[...], preferred_element_type=F32)
    acc = acc + jnp.dot(a2_ref[...], w2_ref[...], preferred_element_type=F32)
    o_ref[...] = x_ref[...] + gate_ref[0] * acc
    _cast_side(refs[:n_side] + refs[n_side + 1:])


def _out_proj(a1, a2, w, x, mod3, row_of, side=()):
    t, k1 = a1.shape
    d = w.shape[1]
    assert a2.shape == (t, k1) and w.shape[0] == 2 * k1
    bm, bn = _pick(t, 1024, 8), _pick(d, 512 if side else 1024)
    gi, gj = t // bm, d // bn
    gate_blk = 2 * gj
    side_specs, side_shapes = _side_casts(side, gi, gj)
    outs = pl.pallas_call(
        _out_proj_kernel,
        grid=(gi, gj),
        in_specs=[pl.BlockSpec((bm, k1), lambda i, j: (i, 0)),
                  pl.BlockSpec((bm, k1), lambda i, j: (i, 0)),
                  pl.BlockSpec((k1, bn), lambda i, j: (0, j)),
                  pl.BlockSpec((k1, bn), lambda i, j: (1, j)),
                  pl.BlockSpec((bm, bn), lambda i, j: (i, j)),
                  pl.BlockSpec((1, 1, bn), lambda i, j: (row_of(i * bm), 0, gate_blk + j))] + side_specs,
        out_specs=[pl.BlockSpec((bm, bn), lambda i, j: (i, j))] + side_specs,
        out_shape=[jax.ShapeDtypeStruct((t, d), F32)] + side_shapes,
        compiler_params=_cparams("arbitrary", "arbitrary"),
    )(a1, a2, w, w, x, mod3, *side)
    return outs[0], list(outs[1:])


def _dft_cos_sin(n):
    idx = np.arange(n, dtype=np.int64)
    ang = 2.0 * np.pi * ((idx[:, None] * idx[None, :]) % n) / n
    return np.cos(ang) / np.sqrt(n), np.sin(ang) / np.sqrt(n)


def _dft_fold_kernel(c_ref, s_ref, wf_ref, o_ref):
    cg = wf_ref.shape[1]
    hi = lax.Precision.HIGHEST
    o_ref[0, :, 0:cg] = jnp.dot(c_ref[...], wf_ref[0], precision=hi, preferred_element_type=F32).astype(BF16)
    o_ref[0, :, cg:] = jnp.dot(s_ref[...], wf_ref[0], precision=hi, preferred_element_type=F32).astype(BF16)


def _dft_fold(wf):
    groups, cg, _ = wf.shape
    cc, sc = _dft_cos_sin(cg)
    const = pl.BlockSpec((cg, cg), lambda g: (0, 0))
    return pl.pallas_call(
        _dft_fold_kernel,
        grid=(groups,),
        in_specs=[const, const, pl.BlockSpec((1, cg, cg), lambda g: (g, 0, 0))],
        out_specs=pl.BlockSpec((1, cg, 2 * cg), lambda g: (g, 0, 0)),
        out_shape=jax.ShapeDtypeStruct((groups, cg, 2 * cg), BF16),
        compiler_params=_cparams("parallel"),
    )(jnp.asarray(cc, F32), jnp.asarray(sc, F32), wf)


def _fourier_kernel(u_ref, gp_ref, cw_ref, csl_ref, jsh_ref, o_ref, q_scr, mid_scr, *, seq, cg, fold):
    if not fold:
        @pl.when(pl.program_id(2) == 0)
        def _():
            for s in range(u_ref.shape[0] // seq):
                p = jnp.dot(u_ref[s * seq:(s + 1) * seq, :], cw_ref[0], preferred_element_type=F32)
                q_scr[s, 0:seq, :] = p[:, :cg].astype(BF16)
                q_scr[s, seq:2 * seq, :] = p[:, cg:].astype(BF16)

        for s in range(u_ref.shape[0] // seq):
            rows = slice(s * csl_ref.shape[0], (s + 1) * csl_ref.shape[0])
            y = jnp.dot(csl_ref[...], q_scr[s], preferred_element_type=F32)
            o_ref[rows, :] = (y * _silu(gp_ref[rows, :].astype(F32))).astype(o_ref.dtype)
        return

    half = seq // 2
    rb = jsh_ref.shape[0]
    nblk = half // rb
    n_seq = u_ref.shape[0] // seq
    tr = csl_ref.shape[0]

    @pl.when(pl.program_id(2) == 0)
    def _():
        row0 = lax.broadcasted_iota(jnp.int32, (rb, 2 * cg), 0) == 0
        sign = jnp.where(lax.broadcasted_iota(jnp.int32, (1, 2 * cg), 1) < cg, 1.0, -1.0)
        for s in range(n_seq):
            p = jnp.dot(u_ref[s * seq:(s + 1) * seq, :], cw_ref[0], preferred_element_type=F32)
            mid_scr[s] = p[half:half + 8, :cg]
            upper = p[half:, :].astype(BF16)
            for a in range(nblk):
                rev = jnp.dot(jsh_ref[...], upper[(nblk - 1 - a) * rb:(nblk - a) * rb, :],
                              preferred_element_type=F32)
                if a > 0:
                    first = half + (nblk - a) * rb
                    rev = jnp.where(row0, p[first:first + 1, :], rev)
                q = p[a * rb:(a + 1) * rb, :] + sign * rev
                q_scr[s, a * rb:(a + 1) * rb, :] = q[:, :cg].astype(BF16)
                q_scr[s, half + a * rb:half + (a + 1) * rb, :] = q[:, cg:].astype(BF16)

    odd_row = (lax.broadcasted_iota(jnp.int32, (tr, 1), 0) & 1) == 1
    cos_mid = jnp.where(odd_row, -1.0, 1.0) * seq ** -0.5
    for s in range(n_seq):
        rows = slice(s * tr, (s + 1) * tr)
        y = jnp.dot(csl_ref[...], q_scr[s], preferred_element_type=F32) + cos_mid * mid_scr[s, 0:1, :]
        o_ref[rows, :] = (y * _silu(gp_ref[rows, :].astype(F32))).astype(o_ref.dtype)


def _fourier(u, cw, nb, seq, gp_col0):
    groups, cg, _ = cw.shape
    t = u.shape[0]
    tr = _pick(seq, 512, 8)
    rt = seq // tr
    spb = 1
    if rt == 1:
        spb = max(1, min(nb, FOURIER_BLOCK_ROWS // seq))
        while nb % spb:
            spb -= 1
    fold = seq >= FOURIER_FOLD_MIN_SEQ
    half = seq // 2
    assert tr % 2 == 0 and half % 16 == 0
    cl, sl = _dft_cos_sin(seq)
    kpos = half if fold else seq
    csl = jnp.asarray(np.concatenate([cl[:, :kpos], -sl[:, :kpos]], axis=1), BF16)
    rb = _pick(half, FOURIER_REV_ROWS, 16)
    jsh = np.zeros((rb, rb), np.float32)
    jsh[np.arange(1, rb), rb - np.arange(1, rb)] = 1.0
    gp_blk0 = gp_col0 // cg
    return pl.pallas_call(
        functools.partial(_fourier_kernel, seq=seq, cg=cg, fold=fold),
        grid=(nb // spb, groups, rt),
        in_specs=[pl.BlockSpec((spb * seq, cg), lambda b, g, r: (b, g)),
                  pl.BlockSpec((spb * tr, cg), lambda b, g, r: (b * rt + r, gp_blk0 + g)),
                  pl.BlockSpec((1, cg, 2 * cg), lambda b, g, r: (g, 0, 0)),
                  pl.BlockSpec((tr, 2 * kpos), lambda b, g, r: (r, 0)),
                  pl.BlockSpec((rb, rb), lambda b, g, r: (0, 0))],
        out_specs=pl.BlockSpec((spb * tr, cg), lambda b, g, r: (b * rt + r, g)),
        out_shape=jax.ShapeDtypeStruct((t, groups * cg), BF16),
        scratch_shapes=[pltpu.VMEM((spb, 2 * kpos, cg), BF16), pltpu.VMEM((spb, 8, cg), F32)],
        compiler_params=_cparams("parallel", "parallel", "arbitrary"),
    )(u, u, cw, csl, jnp.asarray(jsh, BF16))


def _pool_tables(seq):
    r = np.arange(POOL_TILE)[:, None]
    s = np.arange(POOL_TILE + 2 * POOL_HALO)[None, :] - POOL_HALO
    tpos = np.arange(seq)
    bands, cnts = [], []
    for win in POOL_WINDOWS:
        half = win // 2
        assert half <= POOL_HALO
        bands.append(((s >= r - half) & (s <= r + half - 1)).astype(np.float32))
        cnts.append(np.minimum(tpos + half, seq) - np.maximum(tpos - half, 0))
    return np.stack(bands), np.stack(cnts).astype(np.float32)[:, :, None]


def _pool_kernel(u_ref, gp_ref, band_ref, cnt_ref, w_ref, ps_ref, o_ref, *, seq):
    tr = POOL_TILE
    for s0 in range(0, u_ref.shape[0], seq):
        for t in range(seq // tr):
            base = tr * t - POOL_HALO
            lo, hi = max(base, 0), min(base + tr + 2 * POOL_HALO, seq)
            rows = slice(s0 + tr * t, s0 + tr * (t + 1))
            wsum = jnp.dot(band_ref[0, :, lo - base:hi - base], u_ref[s0 + lo:s0 + hi, :],
                           preferred_element_type=F32)
            cnt = cnt_ref[0, tr * t:tr * (t + 1), :]
            d = (wsum / cnt - u_ref[rows, :].astype(F32)).astype(BF16)
            y = jnp.dot(d, w_ref[0], preferred_element_type=F32) * ps_ref[...]
            o_ref[rows, :] = (y * _silu(gp_ref[rows, :].astype(F32))).astype(o_ref.dtype)


def _pool(u, wp, pool_scale, nb, seq, in_col0, gp_col0):
    groups, cg, _ = wp.shape
    assert groups == len(POOL_WINDOWS) and seq % POOL_TILE == 0
    t = u.shape[0]
    band, cnt = _pool_tables(seq)
    in_blk0, gp_blk0 = in_col0 // cg, gp_col0 // cg
    spb = max(1, min(nb, POOL_BLOCK_ROWS // seq))
    while nb % spb:
        spb -= 1
    rows = spb * seq
    return pl.pallas_call(
        functools.partial(_pool_kernel, seq=seq),
        grid=(nb // spb, groups),
        in_specs=[pl.BlockSpec((rows, cg), lambda b, g: (b, in_blk0 + g)),
                  pl.BlockSpec((rows, cg), lambda b, g: (b, gp_blk0 + g)),
                  pl.BlockSpec((1,) + band.shape[1:], lambda b, g: (g, 0, 0)),
                  pl.BlockSpec((1, seq, 1), lambda b, g: (g, 0, 0)),
                  pl.BlockSpec((1, cg, cg), lambda b, g: (g, 0, 0)),
                  pl.BlockSpec((1, cg), lambda b, g: (0, g))],
        out_specs=pl.BlockSpec((rows, cg), lambda b, g: (b, g)),
        out_shape=jax.ShapeDtypeStruct((t, groups * cg), BF16),
        compiler_params=_cparams("parallel", "parallel"),
    )(u, u, jnp.asarray(band, BF16), jnp.asarray(cnt, F32), wp, pool_scale.reshape(1, groups * cg))


def _ctx_attn_kernel(q_ref, k_ref, v_ref, gp_ref, qg_ref, kg_ref, o_ref, sk_ref, sv_ref, *, heads, hd):
    qscale = hd ** -0.5 * LOG2E
    for h in range(heads):
        cols = slice(h * hd, (h + 1) * hd)
        qn = _rms(q_ref[:, cols].astype(F32), qg_ref[...]) * qscale
        kn = _rms(k_ref[:, cols].astype(F32), kg_ref[...])
        v = v_ref[:, cols]
        sk_ref[0, 0, h] = kn
        sv_ref[0, 0, h] = v.astype(F32)
        s = lax.dot_general(qn.astype(BF16), kn.astype(BF16), (((1,), (1,)), ((), ())),
                            preferred_element_type=F32)
        p = jnp.exp2(s - jnp.max(s, axis=-1, keepdims=True))
        v_ones = jnp.concatenate([v, jnp.ones_like(v)], axis=1)
        o = jnp.dot(p.astype(BF16), v_ones, preferred_element_type=F32)
        o = o[:, :hd] / o[:, hd:]
        o_ref[:, cols] = (o * _silu(gp_ref[:, cols].astype(F32))).astype(o_ref.dtype)


def _ctx_attn(u, qg, kg, nb, seq, d_attn, gp_col0):
    hd = qg.shape[-1]
    heads = d_attn // hd
    t = u.shape[0]
    state = jax.ShapeDtypeStruct((nb, 1, heads, seq, hd), F32)
    state_spec = pl.BlockSpec((1, 1, heads, seq, hd), lambda b: (b, 0, 0, 0, 0))
    gp_blk = gp_col0 // d_attn
    return pl.pallas_call(
        functools.partial(_ctx_attn_kernel, heads=heads, hd=hd),
        grid=(nb,),
        in_specs=[pl.BlockSpec((seq, d_attn), lambda b: (b, 0)),
                  pl.BlockSpec((seq, d_attn), lambda b: (b, 1)),
                  pl.BlockSpec((seq, d_attn), lambda b: (b, 2)),
                  pl.BlockSpec((seq, d_attn), lambda b: (b, gp_blk)),
                  pl.BlockSpec((1, hd), lambda b: (0, 0)),
                  pl.BlockSpec((1, hd), lambda b: (0, 0))],
        out_specs=[pl.BlockSpec((seq, d_attn), lambda b: (b, 0)), state_spec, state_spec],
        out_shape=[jax.ShapeDtypeStruct((t, d_attn), BF16), state, state],
        compiler_params=_cparams("parallel"),
    )(u, u, u, u, qg.reshape(1, hd), kg.reshape(1, hd))


def _nbr_plan(rows, kh, n_kh):
    assert rows % NA_QROWS == 0 and rows >= NA_KROWS
    n_dr = 2 * n_kh - 1
    blocks, types = [], []
    for r0 in range(0, rows, NA_QROWS):
        ws = int(np.clip(r0 - kh // 2, 0, rows - NA_KROWS))
        idx = np.full((NA_QROWS, NA_KROWS), n_dr, np.int32)
        for i in range(NA_QROWS):
            r = r0 + i
            start = int(np.clip(r - kh // 2, 0, rows - kh))
            assert ws <= start and start + kh <= ws + NA_KROWS
            for j in range(NA_KROWS):
                rk = ws + j
                if start <= rk < start + kh:
                    idx[i, j] = rk - r + n_kh - 1
        for t, known in enumerate(types):
            if np.array_equal(known, idx):
                break
        else:
            t = len(types)
            types.append(idx)
        blocks.append((r0, ws, t))
    return tuple(blocks), np.stack(types)


def _nbr_build_bias(rb_ref, bias_scr, head, blk_idx, n_dr, n_dc, kw):
    w = GRID_W
    c = lax.broadcasted_iota(jnp.int32, (w, 2 * w), 0)
    lane = lax.broadcasted_iota(jnp.int32, (w, 2 * w), 1)
    cp = lane % w
    q_start = jnp.clip(c - kw // 2, 0, w - kw)
    diff = jnp.where(cp >= q_start, jnp.where(cp < q_start + kw, cp - c + (kw - 1), -1), -1)
    neg = jnp.full((w, 2 * w), NEG_INF, F32)
    pairs = []
    for dr in range(n_dr):
        blk = neg
        for d in range(n_dc):
            blk = jnp.where(diff == d, rb_ref[(head * n_dr + dr) * n_dc + d] * LOG2E, blk)
        pairs.append(blk)
    pairs.append(neg)
    low = lane < w
    nt, qr, kr = blk_idx.shape
    for t in range(nt):
        for i in range(qr):
            for j in range(0, kr, 2):
                e, o = int(blk_idx[t, i, j]), int(blk_idx[t, i, j + 1])
                blk = pairs[e] if e == o else jnp.where(low, pairs[e], pairs[o])
                bias_scr[t, i * w:(i + 1) * w, j * w:(j + 2) * w] = blk


def _nbr_attn_kernel(rb_ref, q_ref, k_ref, v_ref, ck_ref, cv_ref, gp_ref, qg_ref, kg_ref, o_ref,
                     qn_scr, kn_scr, v1_scr, bias_scr, *, blocks, blk_idx, hd, kw, n_dr, n_dc):
    @pl.when(pl.program_id(1) == 0)
    def _():
        _nbr_build_bias(rb_ref, bias_scr, pl.program_id(0), blk_idx, n_dr, n_dc, kw)

    qscale = hd ** -0.5 * LOG2E
    qb, kb = NA_QROWS * GRID_W, NA_KROWS * GRID_W
    nt_dims = (((1,), (1,)), ((), ()))
    qn_scr[...] = (_rms(q_ref[...].astype(F32), qg_ref[...]) * qscale).astype(BF16)
    kn_scr[...] = _rms(k_ref[...].astype(F32), kg_ref[...]).astype(BF16)
    ck = ck_ref[0, 0, 0].astype(BF16)
    cv = cv_ref[0, 0, 0].astype(BF16)
    cv1 = jnp.concatenate([cv, jnp.ones_like(cv)], axis=1)
    v1_scr[:, 0:hd] = v_ref[...]
    v1_scr[:, hd:] = jnp.ones(v_ref.shape, BF16)

    def logits(n):
        r0, ws, t = blocks[n]
        q = qn_scr[r0 * GRID_W:r0 * GRID_W + qb, :]
        s_loc = lax.dot_general(q, kn_scr[ws * GRID_W:ws * GRID_W + kb, :], nt_dims,
                                preferred_element_type=F32) + bias_scr[t]
        return s_loc, lax.dot_general(q, ck, nt_dims, preferred_element_type=F32)

    def softmax(s_loc, s_ctx):
        m = jnp.maximum(jnp.max(s_loc, axis=-1, keepdims=True), jnp.max(s_ctx, axis=-1, keepdims=True))
        p_loc = jnp.exp2(s_loc - m)
        p_ctx = jnp.exp2(s_ctx - m)
        return p_loc.astype(BF16), p_ctx.astype(BF16)

    def attend(n, p_loc, p_ctx):
        r0, ws, _ = blocks[n]
        qs = slice(r0 * GRID_W, r0 * GRID_W + qb)
        o = jnp.dot(p_loc, v1_scr[ws * GRID_W:ws * GRID_W + kb, :], preferred_element_type=F32)
        o = o + jnp.dot(p_ctx, cv1, preferred_element_type=F32)
        o = o[:, :hd] / o[:, hd:]
        o_ref[qs, :] = (o * _silu(gp_ref[qs, :].astype(F32))).astype(o_ref.dtype)

    s_cur, p_prev = logits(0), None
    for n in range(len(blocks)):
        p_cur = softmax(*s_cur)
        if p_prev is not None:
            attend(n - 1, *p_prev)
        if n + 1 < len(blocks):
            s_cur = logits(n + 1)
        p_prev = p_cur
    attend(len(blocks) - 1, *p_prev)


def _nbr_attn(u, cache_k, cache_v, layer, rel_bias, qg, kg, nb, seq, d_attn, gp_col0):
    hd = qg.shape[-1]
    heads = d_attn // hd
    t = u.shape[0]
    past = cache_k.shape[3]
    rows = seq // GRID_W
    _, n_dr, n_dc = rel_bias.shape
    n_kh, kw = (n_dr + 1) // 2, (n_dc + 1) // 2
    assert 2 * GRID_W == LANES and kw <= GRID_W
    blocks, blk_idx = _nbr_plan(rows, min(n_kh, rows), n_kh)
    gp_blk0 = gp_col0 // hd
    cache_spec = pl.BlockSpec((1, 1, 1, past, hd), lambda h, b: (b, layer, h, 0, 0))
    return pl.pallas_call(
        functools.partial(_nbr_attn_kernel, blocks=blocks, blk_idx=blk_idx, hd=hd, kw=kw, n_dr=n_dr,
                          n_dc=n_dc),
        grid=(heads, nb),
        in_specs=[pl.BlockSpec(memory_space=pltpu.SMEM),
                  pl.BlockSpec((seq, hd), lambda h, b: (b, h)),
                  pl.BlockSpec((seq, hd), lambda h, b: (b, heads + h)),
                  pl.BlockSpec((seq, hd), lambda h, b: (b, 2 * heads + h)),
                  cache_spec, cache_spec,
                  pl.BlockSpec((seq, hd), lambda h, b: (b, gp_blk0 + h)),
                  pl.BlockSpec((1, hd), lambda h, b: (0, 0)),
                  pl.BlockSpec((1, hd), lambda h, b: (0, 0))],
        out_specs=pl.BlockSpec((seq, hd), lambda h, b: (b, h)),
        out_shape=jax.ShapeDtypeStruct((t, d_attn), BF16),
        scratch_shapes=[pltpu.VMEM((seq, hd), BF16), pltpu.VMEM((seq, hd), BF16),
                        pltpu.VMEM((seq, 2 * hd), BF16),
                        pltpu.VMEM((blk_idx.shape[0], NA_QROWS * GRID_W, NA_KROWS * GRID_W), F32)],
        compiler_params=_cparams("arbitrary", "arbitrary"),
    )(rel_bias.reshape(-1), u, u, u, cache_k, cache_v, u, qg.reshape(1, hd), kg.reshape(1, hd))


def _conv_kernel(a_ref, b_ref, ap_ref, bp_ref, an_ref, bn_ref, gp_ref, dw_ref, dwb_ref, lng_ref, lnb_ref,
                 wpw_ref, o_ref, h_scr, c_scr, z_scr, *, tiles_per_seq, width):
    tm, dc = a_ref.shape
    nc = dc // LANES
    pos = pl.program_id(0) % tiles_per_seq
    keep_prev = (pos > 0).astype(F32)
    keep_next = (pos < tiles_per_seq - 1).astype(F32)
    for c in range(nc):
        cols = slice(c * LANES, (c + 1) * LANES)
        h_scr[c, 0:CONV_HALO, :] = keep_prev * (
            ap_ref[:, cols].astype(F32) * _sigmoid(bp_ref[:, cols].astype(F32)))
        h_scr[c, CONV_HALO:CONV_HALO + tm, :] = (
            a_ref[:, cols].astype(F32) * _sigmoid(b_ref[:, cols].astype(F32)))
        h_scr[c, CONV_HALO + tm:, :] = keep_next * (
            an_ref[:, cols].astype(F32) * _sigmoid(bn_ref[:, cols].astype(F32)))

    off = CONV_HALO - width // 2

    def chunk(c, carry):
        for rb in range(tm // CONV_ROWS):
            acc = jnp.zeros((CONV_ROWS, LANES), F32)
            for k in range(width):
                acc = acc + dw_ref[c, k:k + 1, :] * h_scr[c, pl.ds(rb * CONV_ROWS + off + k, CONV_ROWS), :]
            c_scr[c, rb * CONV_ROWS:(rb + 1) * CONV_ROWS, :] = acc + dwb_ref[c]
        return carry

    lax.fori_loop(0, nc, chunk, 0)

    s1 = jnp.zeros((tm, LANES), F32)
    for c in range(nc):
        s1 = s1 + c_scr[c]
    mu = jnp.sum(s1, axis=-1, keepdims=True) / dc
    s2 = jnp.zeros((tm, LANES), F32)
    for c in range(nc):
        xc = c_scr[c] - mu
        s2 = s2 + xc * xc
    rstd = lax.rsqrt(jnp.sum(s2, axis=-1, keepdims=True) / dc + EPS)
    for c in range(nc):
        cols = slice(c * LANES, (c + 1) * LANES)
        y = (c_scr[c] - mu) * rstd * lng_ref[:, cols] + lnb_ref[:, cols]
        z_scr[:, cols] = _silu(y).astype(BF16)
    out = jnp.dot(z_scr[...], wpw_ref[...], preferred_element_type=F32)
    o_ref[...] = (out * _silu(gp_ref[...].astype(F32))).astype(o_ref.dtype)


def _conv_module(u, dw, dw_b, ln_g, ln_b, w_pw, seq, in_col0, gp_col0):
    width, dc = dw.shape
    t = u.shape[0]
    tm = CONV_TILE
    assert seq % tm == 0 and width // 2 <= CONV_HALO and tm % CONV_HALO == 0 and dc % LANES == 0
    nc = dc // LANES
    tiles_per_seq = seq // tm
    hb = tm // CONV_HALO
    n_hblk = t // CONV_HALO
    a_blk, gp_blk = in_col0 // dc, gp_col0 // dc
    wpad = -width % 8
    dw3 = jnp.pad(dw, ((0, wpad), (0, 0))).reshape(width + wpad, nc, LANES).transpose(1, 0, 2)
    prev = lambda i: jnp.maximum(i * hb - 1, 0)
    nxt = lambda i: jnp.minimum((i + 1) * hb, n_hblk - 1)
    vec = lambda x: x.reshape(1, dc)
    return pl.pallas_call(
        functools.partial(_conv_kernel, tiles_per_seq=tiles_per_seq, width=width),
        grid=(t // tm,),
        in_specs=[pl.BlockSpec((tm, dc), lambda i: (i, a_blk)),
                  pl.BlockSpec((tm, dc), lambda i: (i, a_blk + 1)),
                  pl.BlockSpec((CONV_HALO, dc), lambda i: (prev(i), a_blk)),
                  pl.BlockSpec((CONV_HALO, dc), lambda i: (prev(i), a_blk + 1)),
                  pl.BlockSpec((CONV_HALO, dc), lambda i: (nxt(i), a_blk)),
                  pl.BlockSpec((CONV_HALO, dc), lambda i: (nxt(i), a_blk + 1)),
                  pl.BlockSpec((tm, dc), lambda i: (i, gp_blk)),
                  pl.BlockSpec((nc, width + wpad, LANES), lambda i: (0, 0, 0)),
                  pl.BlockSpec((nc, 1, LANES), lambda i: (0, 0, 0)),
                  pl.BlockSpec((1, dc), lambda i: (0, 0)),
                  pl.BlockSpec((1, dc), lambda i: (0, 0)),
                  pl.BlockSpec((dc, dc), lambda i: (0, 0))],
        out_specs=pl.BlockSpec((tm, dc), lambda i: (i, 0)),
        out_shape=jax.ShapeDtypeStruct((t, dc), BF16),
        scratch_shapes=[pltpu.VMEM((nc, tm + 2 * CONV_HALO, LANES), F32),
                        pltpu.VMEM((nc, tm, LANES), F32),
                        pltpu.VMEM((tm, dc), BF16)],
        compiler_params=_cparams("parallel"),
    )(u, u, u, u, u, u, u, dw3, dw_b.reshape(nc, 1, LANES), vec(ln_g), vec(ln_b), w_pw)


def _with_casts(fn, wanted, wts, bf16):
    todo = [key for key in wanted if key not in bf16]
    out, copies = fn(side=[wts[name][jj] for name, jj in todo])
    bf16.update(zip(todo, copies))
    return out


def _in_proj(h, i, wts, bf16):
    j = i // 2
    own = ("w_in_even", j) if i % 2 == 0 else ("w_in_odd", j)
    wanted = [("w_out_even", j)] if i % 2 == 0 else [("w_out_odd", j), ("w_conv_pw", j)]
    if i + 1 < wts["norm_g"].shape[0]:
        wanted.append(("w_in_odd" if i % 2 == 0 else "w_in_even", (i + 1) // 2))
    if own not in bf16:
        bf16[own] = wts[own[0]][own[1]].astype(BF16)
    return _with_casts(functools.partial(_matmul, h, bf16[own]), wanted, wts, bf16)


def _layer_out(y1, y2, x, mod3, row_of, i, wts, bf16):
    own = ("w_out_even", i // 2) if i % 2 == 0 else ("w_out_odd", i // 2)
    return _with_casts(functools.partial(_out_proj, y1, y2, bf16[own], x, mod3, row_of), [], wts, bf16)


def _trunk(x, nb, seq, row_of, mod, caches, wts, bf16):
    d = x.shape[1]
    depth = mod.shape[0]
    new_k, new_v = [], []
    for i in range(depth):
        mod3 = mod[i][:, None, :]
        h = _norm_mod(x, wts["norm_g"][i], mod3, row_of)
        j = i // 2
        if i % 2 == 0:
            u = _in_proj(h, i, wts, bf16)
            d_f = wts["w_fourier"].shape[1] * wts["w_fourier"].shape[2]
            d_inner = d_f + wts["w_pool"].shape[1] * wts["w_pool"].shape[2]
            y1 = _fourier(u, wts["dft_w"][j], nb, seq, gp_col0=d_inner)
            y2 = _pool(u, wts["w_pool"][j], wts["pool_scale"][j], nb, seq, in_col0=d_f,
                       gp_col0=d_inner + d_f)
            x = _layer_out(y1, y2, x, mod3, row_of, i, wts, bf16)
        else:
            u = _in_proj(h, i, wts, bf16)
            d_conv = wts["conv_dw"].shape[2]
            d_attn = (u.shape[1] - 3 * d_conv) // 4
            gp_col0 = 3 * d_attn + 2 * d_conv
            if caches is None:
                y1, sk, sv = _ctx_attn(u, wts["q_norm_g"][j], wts["k_norm_g"][j], nb, seq, d_attn, gp_col0)
                new_k.append(sk)
                new_v.append(sv)
            else:
                y1 = _nbr_attn(u, caches[0], caches[1], j, wts["rel_bias"][j], wts["q_norm_g"][j],
                               wts["k_norm_g"][j], nb, seq, d_attn, gp_col0)
            y2 = _conv_module(u, wts["conv_dw"][j], wts["conv_dw_b"][j], wts["conv_ln_g"][j],
                              wts["conv_ln_b"][j], bf16["w_conv_pw", j], seq, in_col0=3 * d_attn,
                              gp_col0=gp_col0 + d_attn)
            x = _layer_out(y1, y2, x, mod3, row_of, i, wts, bf16)
    return x, new_k, new_v


def kernel(x_prompt, x_sample, cache_k, cache_v, c, c_ctx, norm_g, w_ada, b_ada, w_in_even, w_out_even,
           w_fourier, w_pool, pool_scale, w_in_odd, w_out_odd, q_norm_g, k_norm_g, rel_bias, conv_dw,
           conv_dw_b, conv_ln_g, conv_ln_b, w_conv_pw):
    nb_p, seq_p, d = x_prompt.shape
    nb_s, seq_s, _ = x_sample.shape
    n_rows = -(-(1 + nb_s) // 8) * 8
    cond = jnp.concatenate([c_ctx[None, :], c, jnp.zeros((n_rows - 1 - nb_s, d), F32)], axis=0)
    mod = _adaln(cond, w_ada, b_ada)
    wts = dict(norm_g=norm_g, pool_scale=pool_scale, q_norm_g=q_norm_g, k_norm_g=k_norm_g,
               rel_bias=rel_bias, conv_dw=conv_dw, conv_dw_b=conv_dw_b, conv_ln_g=conv_ln_g,
               conv_ln_b=conv_ln_b, w_in_even=w_in_even, w_out_even=w_out_even, w_in_odd=w_in_odd,
               w_out_odd=w_out_odd, w_conv_pw=w_conv_pw, w_fourier=w_fourier,
               w_pool=w_pool.astype(BF16))
    wts["dft_w"] = [_dft_fold(w_fourier[j]) for j in range(w_fourier.shape[0])]
    bf16 = {}
    y_p, ks, vs = _trunk(x_prompt.reshape(nb_p * seq_p, d), nb_p, seq_p, lambda r: 0, mod, None, wts, bf16)
    y_s, _, _ = _trunk(x_sample.reshape(nb_s * seq_s, d), nb_s, seq_s, lambda r: 1 + r // seq_s, mod,
                       (cache_k, cache_v), wts, bf16)
    state_k = jnp.concatenate(ks, axis=1)
    state_v = jnp.concatenate(vs, axis=1)
    return (y_p.reshape(nb_p, seq_p, d), y_s.reshape(nb_s, seq_s, d), state_k, state_v)
```

```python
import functools

import numpy as np
import jax
import jax.numpy as jnp
from jax import lax
from jax.experimental import pallas as pl
from jax.experimental.pallas import tpu as pltpu

F32 = jnp.float32
BF16 = jnp.bfloat16

GRID_W = 64
POOL_WINDOWS = (2, 4, 8, 16)
EPS = 1e-6
NEG_INF = -1e30
LOG2E = float(np.log2(np.e))

LANES = 128
V7X_VMEM_LIMIT_BYTES = 60 * 1024 * 1024

NORM_ROWS = 16

NA_QROWS = 2
NA_KROWS = 10
POOL_TILE = 256
POOL_HALO = 128
POOL_BLOCK_ROWS = 2048
FOURIER_BLOCK_ROWS = 1024
FOURIER_REV_ROWS = 256
FOURIER_FOLD_MIN_SEQ = 1024
CONV_TILE = 256
CONV_HALO = 16
CONV_ROWS = 64


def _cparams(*sem):
    return pltpu.CompilerParams(dimension_semantics=sem, vmem_limit_bytes=V7X_VMEM_LIMIT_BYTES)


def _pick(n, pref, unit=LANES):
    if n <= pref:
        return n
    t = (pref // unit) * unit
    while t > unit and n % t:
        t -= unit
    assert n % t == 0, (n, pref, unit)
    return t


def _sigmoid(x):
    return 0.5 * jnp.tanh(0.5 * x) + 0.5


def _silu(x):
    return x * _sigmoid(x)


def _rms(x, g):
    return x * lax.rsqrt(jnp.mean(x * x, axis=-1, keepdims=True) + EPS) * g


def _adaln_kernel(cond_ref, w_ref, b_ref, o_ref):
    s = _silu(cond_ref[...]).astype(BF16)
    o_ref[0] = jnp.dot(s, w_ref[0].astype(BF16), preferred_element_type=F32) + b_ref[0]


def _adaln(cond, w_ada, b_ada):
    depth, d, n = w_ada.shape
    r = cond.shape[0]
    tn = _pick(n, 1024)
    return pl.pallas_call(
        _adaln_kernel,
        grid=(depth, n // tn),
        in_specs=[pl.BlockSpec((r, d), lambda l, j: (0, 0)),
                  pl.BlockSpec((1, d, tn), lambda l, j: (l, 0, j)),
                  pl.BlockSpec((1, 1, tn), lambda l, j: (l, 0, j))],
        out_specs=pl.BlockSpec((1, r, tn), lambda l, j: (l, 0, j)),
        out_shape=jax.ShapeDtypeStruct((depth, r, n), F32),
        compiler_params=_cparams("parallel", "parallel"),
    )(cond, w_ada, b_ada.reshape(depth, 1, n))


def _norm_mod_kernel(x_ref, g_ref, shift_ref, scale_ref, o_ref, gain_scr, shift_scr):
    gain_scr[...] = jnp.broadcast_to(g_ref[...] * (1.0 + scale_ref[0]), gain_scr.shape)
    shift_scr[...] = jnp.broadcast_to(shift_ref[0], shift_scr.shape)

    def rows16(r, carry):
        rows = pl.ds(pl.multiple_of(r * NORM_ROWS, NORM_ROWS), NORM_ROWS)
        x = x_ref[rows, :]
        rstd = lax.rsqrt(jnp.mean(x * x, axis=-1, keepdims=True) + EPS)
        o_ref[rows, :] = (x * rstd * gain_scr[...] + shift_scr[...]).astype(o_ref.dtype)
        return carry

    lax.fori_loop(0, x_ref.shape[0] // NORM_ROWS, rows16, 0, unroll=4)


def _norm_mod(x, g, mod3, row_of):
    t, d = x.shape
    tm = _pick(t, 512, 8)
    return pl.pallas_call(
        _norm_mod_kernel,
        grid=(t // tm,),
        in_specs=[pl.BlockSpec((tm, d), lambda i: (i, 0)),
                  pl.BlockSpec((1, d), lambda i: (0, 0)),
                  pl.BlockSpec((1, 1, d), lambda i: (row_of(i * tm), 0, 0)),
                  pl.BlockSpec((1, 1, d), lambda i: (row_of(i * tm), 0, 1))],
        out_specs=pl.BlockSpec((tm, d), lambda i: (i, 0)),
        out_shape=jax.ShapeDtypeStruct((t, d), BF16),
        scratch_shapes=[pltpu.VMEM((NORM_ROWS, d), F32), pltpu.VMEM((NORM_ROWS, d), F32)],
        compiler_params=_cparams("parallel"),
    )(x, g.reshape(1, d), mod3, mod3)


def _side_casts(side, gi, gj):
    specs, shapes = [], []
    for s in side:
        rows, cols = s.shape
        nblk = min(gi * gj, rows // 16)
        while rows % nblk or (rows // nblk) % 16:
            nblk -= 1
        specs.append(pl.BlockSpec((rows // nblk, cols),
                                  lambda i, j, nblk=nblk: (jnp.minimum(i * gj + j, nblk - 1), 0)))
        shapes.append(jax.ShapeDtypeStruct(s.shape, BF16))
    return specs, shapes


def _cast_side(side_refs):
    n = len(side_refs) // 2
    for src_ref, dst_ref in zip(side_refs[:n], side_refs[n:]):
        dst_ref[...] = src_ref[...].astype(dst_ref.dtype)


def _matmul_kernel(a_ref, w_ref, *refs):
    n_side = (len(refs) - 1) // 2
    o_ref = refs[n_side]
    o_ref[...] = jnp.dot(a_ref[...], w_ref[...], preferred_element_type=F32).astype(o_ref.dtype)
    _cast_side(refs[:n_side] + refs[n_side + 1:])


def _matmul(a, w, side=()):
    m, k = a.shape
    n = w.shape[1]
    bm, bn = _pick(m, 1024, 8), _pick(n, 1024)
    gi, gj = m // bm, n // bn
    side_specs, side_shapes = _side_casts(side, gi, gj)
    outs = pl.pallas_call(
        _matmul_kernel,
        grid=(gi, gj),
        in_specs=[pl.BlockSpec((bm, k), lambda i, j: (i, 0)),
                  pl.BlockSpec((k, bn), lambda i, j: (0, j))] + side_specs,
        out_specs=[pl.BlockSpec((bm, bn), lambda i, j: (i, j))] + side_specs,
        out_shape=[jax.ShapeDtypeStruct((m, n), BF16)] + side_shapes,
        compiler_params=_cparams("arbitrary", "arbitrary"),
    )(a, w, *side)
    return outs[0], list(outs[1:])


def _out_proj_kernel(a1_ref, a2_ref, w1_ref, w2_ref, x_ref, gate_ref, *refs):
    n_side = (len(refs) - 1) // 2
    o_ref = refs[n_side]
    acc = jnp.dot(a1_ref[...], w1_ref[...], preferred_element_type=F32)
    acc = acc + jnp.dot(a2_ref[...], w2_ref[...], preferred_element_type=F32)
    o_ref[...] = x_ref[...] + gate_ref[0] * acc
    _cast_side(refs[:n_side] + refs[n_side + 1:])


def _out_proj(a1, a2, w, x, mod3, row_of, side=()):
    t, k1 = a1.shape
    d = w.shape[1]
    assert a2.shape == (t, k1) and w.shape[0] == 2 * k1
    bm, bn = _pick(t, 1024, 8), _pick(d, 512 if side else 1024)
    gi, gj = t // bm, d // bn
    gate_blk = 2 * gj
    side_specs, side_shapes = _side_casts(side, gi, gj)
    outs = pl.pallas_call(
        _out_proj_kernel,
        grid=(gi, gj),
        in_specs=[pl.BlockSpec((bm, k1), lambda i, j: (i, 0)),
                  pl.BlockSpec((bm, k1), lambda i, j: (i, 0)),
                  pl.BlockSpec((k1, bn), lambda i, j: (0, j)),
                  pl.BlockSpec((k1, bn), lambda i, j: (1, j)),
                  pl.BlockSpec((bm, bn), lambda i, j: (i, j)),
                  pl.BlockSpec((1, 1, bn), lambda i, j: (row_of(i * bm), 0, gate_blk + j))] + side_specs,
        out_specs=[pl.BlockSpec((bm, bn), lambda i, j: (i, j))] + side_specs,
        out_shape=[jax.ShapeDtypeStruct((t, d), F32)] + side_shapes,
        compiler_params=_cparams("arbitrary", "arbitrary"),
    )(a1, a2, w, w, x, mod3, *side)
    return outs[0], list(outs[1:])


def _dft_cos_sin(n):
    idx = np.arange(n, dtype=np.int64)
    ang = 2.0 * np.pi * ((idx[:, None] * idx[None, :]) % n) / n
    return np.cos(ang) / np.sqrt(n), np.sin(ang) / np.sqrt(n)


def _dft_fold_kernel(c_ref, s_ref, wf_ref, o_ref):
    cg = wf_ref.shape[1]
    hi = lax.Precision.HIGHEST
    o_ref[0, :, 0:cg] = jnp.dot(c_ref[...], wf_ref[0], precision=hi, preferred_element_type=F32).astype(BF16)
    o_ref[0, :, cg:] = jnp.dot(s_ref[...], wf_ref[0], precision=hi, preferred_element_type=F32).astype(BF16)


def _dft_fold(wf):
    groups, cg, _ = wf.shape
    cc, sc = _dft_cos_sin(cg)
    const = pl.BlockSpec((cg, cg), lambda g: (0, 0))
    return pl.pallas_call(
        _dft_fold_kernel,
        grid=(groups,),
        in_specs=[const, const, pl.BlockSpec((1, cg, cg), lambda g: (g, 0, 0))],
        out_specs=pl.BlockSpec((1, cg, 2 * cg), lambda g: (g, 0, 0)),
        out_shape=jax.ShapeDtypeStruct((groups, cg, 2 * cg), BF16),
        compiler_params=_cparams("parallel"),
    )(jnp.asarray(cc, F32), jnp.asarray(sc, F32), wf)


def _fourier_kernel(u_ref, gp_ref, cw_ref, csl_ref, jsh_ref, o_ref, q_scr, mid_scr, *, seq, cg, fold):
    if not fold:
        @pl.when(pl.program_id(2) == 0)
        def _():
            for s in range(u_ref.shape[0] // seq):
                p = jnp.dot(u_ref[s * seq:(s + 1) * seq, :], cw_ref[0], preferred_element_type=F32)
                q_scr[s, 0:seq, :] = p[:, :cg].astype(BF16)
                q_scr[s, seq:2 * seq, :] = p[:, cg:].astype(BF16)

        for s in range(u_ref.shape[0] // seq):
            rows = slice(s * csl_ref.shape[0], (s + 1) * csl_ref.shape[0])
            y = jnp.dot(csl_ref[...], q_scr[s], preferred_element_type=F32)
            o_ref[rows, :] = (y * _silu(gp_ref[rows, :].astype(F32))).astype(o_ref.dtype)
        return

    half = seq // 2
    rb = jsh_ref.shape[0]
    nblk = half // rb
    n_seq = u_ref.shape[0] // seq
    tr = csl_ref.shape[0]

    @pl.when(pl.program_id(2) == 0)
    def _():
        row0 = lax.broadcasted_iota(jnp.int32, (rb, 2 * cg), 0) == 0
        sign = jnp.where(lax.broadcasted_iota(jnp.int32, (1, 2 * cg), 1) < cg, 1.0, -1.0)
        for s in range(n_seq):
            p = jnp.dot(u_ref[s * seq:(s + 1) * seq, :], cw_ref[0], preferred_element_type=F32)
            mid_scr[s] = p[half:half + 8, :cg]
            upper = p[half:, :].astype(BF16)
            for a in range(nblk):
                rev = jnp.dot(jsh_ref[...], upper[(nblk - 1 - a) * rb:(nblk - a) * rb, :],
                              preferred_element_type=F32)
                if a > 0:
                    first = half + (nblk - a) * rb
                    rev = jnp.where(row0, p[first:first + 1, :], rev)
                q = p[a * rb:(a + 1) * rb, :] + sign * rev
                q_scr[s, a * rb:(a + 1) * rb, :] = q[:, :cg].astype(BF16)
                q_scr[s, half + a * rb:half + (a + 1) * rb, :] = q[:, cg:].astype(BF16)

    odd_row = (lax.broadcasted_iota(jnp.int32, (tr, 1), 0) & 1) == 1
    cos_mid = jnp.where(odd_row, -1.0, 1.0) * seq ** -0.5
    for s in range(n_seq):
        rows = slice(s * tr, (s + 1) * tr)
        y = jnp.dot(csl_ref[...], q_scr[s], preferred_element_type=F32) + cos_mid * mid_scr[s, 0:1, :]
        o_ref[rows, :] = (y * _silu(gp_ref[rows, :].astype(F32))).astype(o_ref.dtype)


def _fourier(u, cw, nb, seq, gp_col0):
    groups, cg, _ = cw.shape
    t = u.shape[0]
    tr = _pick(seq, 512, 8)
    rt = seq // tr
    spb = 1
    if rt == 1:
        spb = max(1, min(nb, FOURIER_BLOCK_ROWS // seq))
        while nb % spb:
            spb -= 1
    fold = seq >= FOURIER_FOLD_MIN_SEQ
    half = seq // 2
    assert tr % 2 == 0 and half % 16 == 0
    cl, sl = _dft_cos_sin(seq)
    kpos = half if fold else seq
    csl = jnp.asarray(np.concatenate([cl[:, :kpos], -sl[:, :kpos]], axis=1), BF16)
    rb = _pick(half, FOURIER_REV_ROWS, 16)
    jsh = np.zeros((rb, rb), np.float32)
    jsh[np.arange(1, rb), rb - np.arange(1, rb)] = 1.0
    gp_blk0 = gp_col0 // cg
    return pl.pallas_call(
        functools.partial(_fourier_kernel, seq=seq, cg=cg, fold=fold),
        grid=(nb // spb, groups, rt),
        in_specs=[pl.BlockSpec((spb * seq, cg), lambda b, g, r: (b, g)),
                  pl.BlockSpec((spb * tr, cg), lambda b, g, r: (b * rt + r, gp_blk0 + g)),
                  pl.BlockSpec((1, cg, 2 * cg), lambda b, g, r: (g, 0, 0)),
                  pl.BlockSpec((tr, 2 * kpos), lambda b, g, r: (r, 0)),
                  pl.BlockSpec((rb, rb), lambda b, g, r: (0, 0))],
        out_specs=pl.BlockSpec((spb * tr, cg), lambda b, g, r: (b * rt + r, g)),
        out_shape=jax.ShapeDtypeStruct((t, groups * cg), BF16),
        scratch_shapes=[pltpu.VMEM((spb, 2 * kpos, cg), BF16), pltpu.VMEM((spb, 8, cg), F32)],
        compiler_params=_cparams("parallel", "parallel", "arbitrary"),
    )(u, u, cw, csl, jnp.asarray(jsh, BF16))


def _pool_tables(seq):
    r = np.arange(POOL_TILE)[:, None]
    s = np.arange(POOL_TILE + 2 * POOL_HALO)[None, :] - POOL_HALO
    tpos = np.arange(seq)
    bands, cnts = [], []
    for win in POOL_WINDOWS:
        half = win // 2
        assert half <= POOL_HALO
        bands.append(((s >= r - half) & (s <= r + half - 1)).astype(np.float32))
        cnts.append(np.minimum(tpos + half, seq) - np.maximum(tpos - half, 0))
    return np.stack(bands), np.stack(cnts).astype(np.float32)[:, :, None]


def _pool_kernel(u_ref, gp_ref, band_ref, cnt_ref, w_ref, ps_ref, o_ref, *, seq):
    tr = POOL_TILE
    for s0 in range(0, u_ref.shape[0], seq):
        for t in range(seq // tr):
            base = tr * t - POOL_HALO
            lo, hi = max(base, 0), min(base + tr + 2 * POOL_HALO, seq)
            rows = slice(s0 + tr * t, s0 + tr * (t + 1))
            wsum = jnp.dot(band_ref[0, :, lo - base:hi - base], u_ref[s0 + lo:s0 + hi, :],
                           preferred_element_type=F32)
            cnt = cnt_ref[0, tr * t:tr * (t + 1), :]
            d = (wsum / cnt - u_ref[rows, :].astype(F32)).astype(BF16)
            y = jnp.dot(d, w_ref[0], preferred_element_type=F32) * ps_ref[...]
            o_ref[rows, :] = (y * _silu(gp_ref[rows, :].astype(F32))).astype(o_ref.dtype)


def _pool(u, wp, pool_scale, nb, seq, in_col0, gp_col0):
    groups, cg, _ = wp.shape
    assert groups == len(POOL_WINDOWS) and seq % POOL_TILE == 0
    t = u.shape[0]
    band, cnt = _pool_tables(seq)
    in_blk0, gp_blk0 = in_col0 // cg, gp_col0 // cg
    spb = max(1, min(nb, POOL_BLOCK_ROWS // seq))
    while nb % spb:
        spb -= 1
    rows = spb * seq
    return pl.pallas_call(
        functools.partial(_pool_kernel, seq=seq),
        grid=(nb // spb, groups),
        in_specs=[pl.BlockSpec((rows, cg), lambda b, g: (b, in_blk0 + g)),
                  pl.BlockSpec((rows, cg), lambda b, g: (b, gp_blk0 + g)),
                  pl.BlockSpec((1,) + band.shape[1:], lambda b, g: (g, 0, 0)),
                  pl.BlockSpec((1, seq, 1), lambda b, g: (g, 0, 0)),
                  pl.BlockSpec((1, cg, cg), lambda b, g: (g, 0, 0)),
                  pl.BlockSpec((1, cg), lambda b, g: (0, g))],
        out_specs=pl.BlockSpec((rows, cg), lambda b, g: (b, g)),
        out_shape=jax.ShapeDtypeStruct((t, groups * cg), BF16),
        compiler_params=_cparams("parallel", "parallel"),
    )(u, u, jnp.asarray(band, BF16), jnp.asarray(cnt, F32), wp, pool_scale.reshape(1, groups * cg))


def _ctx_attn_kernel(q_ref, k_ref, v_ref, gp_ref, qg_ref, kg_ref, o_ref, sk_ref, sv_ref, *, heads, hd):
    qscale = hd ** -0.5 * LOG2E
    for h in range(heads):
        cols = slice(h * hd, (h + 1) * hd)
        qn = _rms(q_ref[:, cols].astype(F32), qg_ref[...]) * qscale
        kn = _rms(k_ref[:, cols].astype(F32), kg_ref[...])
        v = v_ref[:, cols]
        sk_ref[0, 0, h] = kn
        sv_ref[0, 0, h] = v.astype(F32)
        s = lax.dot_general(qn.astype(BF16), kn.astype(BF16), (((1,), (1,)), ((), ())),
                            preferred_element_type=F32)
        p = jnp.exp2(s - jnp.max(s, axis=-1, keepdims=True))
        v_ones = jnp.concatenate([v, jnp.ones_like(v)], axis=1)
        o = jnp.dot(p.astype(BF16), v_ones, preferred_element_type=F32)
        o = o[:, :hd] / o[:, hd:]
        o_ref[:, cols] = (o * _silu(gp_ref[:, cols].astype(F32))).astype(o_ref.dtype)


def _ctx_attn(u, qg, kg, nb, seq, d_attn, gp_col0):
    hd = qg.shape[-1]
    heads = d_attn // hd
    t = u.shape[0]
    state = jax.ShapeDtypeStruct((nb, 1, heads, seq, hd), F32)
    state_spec = pl.BlockSpec((1, 1, heads, seq, hd), lambda b: (b, 0, 0, 0, 0))
    gp_blk = gp_col0 // d_attn
    return pl.pallas_call(
        functools.partial(_ctx_attn_kernel, heads=heads, hd=hd),
        grid=(nb,),
        in_specs=[pl.BlockSpec((seq, d_attn), lambda b: (b, 0)),
                  pl.BlockSpec((seq, d_attn), lambda b: (b, 1)),
                  pl.BlockSpec((seq, d_attn), lambda b: (b, 2)),
                  pl.BlockSpec((seq, d_attn), lambda b: (b, gp_blk)),
                  pl.BlockSpec((1, hd), lambda b: (0, 0)),
                  pl.BlockSpec((1, hd), lambda b: (0, 0))],
        out_specs=[pl.BlockSpec((seq, d_attn), lambda b: (b, 0)), state_spec, state_spec],
        out_shape=[jax.ShapeDtypeStruct((t, d_attn), BF16), state, state],
        compiler_params=_cparams("parallel"),
    )(u, u, u, u, qg.reshape(1, hd), kg.reshape(1, hd))


def _nbr_plan(rows, kh, n_kh):
    assert rows % NA_QROWS == 0 and rows >= NA_KROWS
    n_dr = 2 * n_kh - 1
    blocks, types = [], []
    for r0 in range(0, rows, NA_QROWS):
        ws = int(np.clip(r0 - kh // 2, 0, rows - NA_KROWS))
        idx = np.full((NA_QROWS, NA_KROWS), n_dr, np.int32)
        for i in range(NA_QROWS):
            r = r0 + i
            start = int(np.clip(r - kh // 2, 0, rows - kh))
            assert ws <= start and start + kh <= ws + NA_KROWS
            for j in range(NA_KROWS):
                rk = ws + j
                if start <= rk < start + kh:
                    idx[i, j] = rk - r + n_kh - 1
        for t, known in enumerate(types):
            if np.array_equal(known, idx):
                break
        else:
            t = len(types)
            types.append(idx)
        blocks.append((r0, ws, t))
    return tuple(blocks), np.stack(types)


def _nbr_build_bias(rb_ref, bias_scr, head, blk_idx, n_dr, n_dc, kw):
    w = GRID_W
    c = lax.broadcasted_iota(jnp.int32, (w, 2 * w), 0)
    lane = lax.broadcasted_iota(jnp.int32, (w, 2 * w), 1)
    cp = lane % w
    q_start = jnp.clip(c - kw // 2, 0, w - kw)
    diff = jnp.where(cp >= q_start, jnp.where(cp < q_start + kw, cp - c + (kw - 1), -1), -1)
    neg = jnp.full((w, 2 * w), NEG_INF, F32)
    pairs = []
    for dr in range(n_dr):
        blk = neg
        for d in range(n_dc):
            blk = jnp.where(diff == d, rb_ref[(head * n_dr + dr) * n_dc + d] * LOG2E, blk)
        pairs.append(blk)
    pairs.append(neg)
    low = lane < w
    nt, qr, kr = blk_idx.shape
    for t in range(nt):
        for i in range(qr):
            for j in range(0, kr, 2):
                e, o = int(blk_idx[t, i, j]), int(blk_idx[t, i, j + 1])
                blk = pairs[e] if e == o else jnp.where(low, pairs[e], pairs[o])
                bias_scr[t, i * w:(i + 1) * w, j * w:(j + 2) * w] = blk


def _nbr_attn_kernel(rb_ref, q_ref, k_ref, v_ref, ck_ref, cv_ref, gp_ref, qg_ref, kg_ref, o_ref,
                     qn_scr, kn_scr, v1_scr, bias_scr, *, blocks, blk_idx, hd, kw, n_dr, n_dc):
    @pl.when(pl.program_id(1) == 0)
    def _():
        _nbr_build_bias(rb_ref, bias_scr, pl.program_id(0), blk_idx, n_dr, n_dc, kw)

    qscale = hd ** -0.5 * LOG2E
    qb, kb = NA_QROWS * GRID_W, NA_KROWS * GRID_W
    nt_dims = (((1,), (1,)), ((), ()))
    qn_scr[...] = (_rms(q_ref[...].astype(F32), qg_ref[...]) * qscale).astype(BF16)
    kn_scr[...] = _rms(k_ref[...].astype(F32), kg_ref[...]).astype(BF16)
    ck = ck_ref[0, 0, 0].astype(BF16)
    cv = cv_ref[0, 0, 0].astype(BF16)
    cv1 = jnp.concatenate([cv, jnp.ones_like(cv)], axis=1)
    v1_scr[:, 0:hd] = v_ref[...]
    v1_scr[:, hd:] = jnp.ones(v_ref.shape, BF16)

    def logits(n):
        r0, ws, t = blocks[n]
        q = qn_scr[r0 * GRID_W:r0 * GRID_W + qb, :]
        s_loc = lax.dot_general(q, kn_scr[ws * GRID_W:ws * GRID_W + kb, :], nt_dims,
                                preferred_element_type=F32) + bias_scr[t]
        return s_loc, lax.dot_general(q, ck, nt_dims, preferred_element_type=F32)

    def softmax(s_loc, s_ctx):
        m = jnp.maximum(jnp.max(s_loc, axis=-1, keepdims=True), jnp.max(s_ctx, axis=-1, keepdims=True))
        p_loc = jnp.exp2(s_loc - m)
        p_ctx = jnp.exp2(s_ctx - m)
        return p_loc.astype(BF16), p_ctx.astype(BF16)

    def attend(n, p_loc, p_ctx):
        r0, ws, _ = blocks[n]
        qs = slice(r0 * GRID_W, r0 * GRID_W + qb)
        o = jnp.dot(p_loc, v1_scr[ws * GRID_W:ws * GRID_W + kb, :], preferred_element_type=F32)
        o = o + jnp.dot(p_ctx, cv1, preferred_element_type=F32)
        o = o[:, :hd] / o[:, hd:]
        o_ref[qs, :] = (o * _silu(gp_ref[qs, :].astype(F32))).astype(o_ref.dtype)

    s_cur, p_prev = logits(0), None
    for n in range(len(blocks)):
        p_cur = softmax(*s_cur)
        if p_prev is not None:
            attend(n - 1, *p_prev)
        if n + 1 < len(blocks):
            s_cur = logits(n + 1)
        p_prev = p_cur
    attend(len(blocks) - 1, *p_prev)


def _nbr_attn(u, cache_k, cache_v, layer, rel_bias, qg, kg, nb, seq, d_attn, gp_col0):
    hd = qg.shape[-1]
    heads = d_attn // hd
    t = u.shape[0]
    past = cache_k.shape[3]
    rows = seq // GRID_W
    _, n_dr, n_dc = rel_bias.shape
    n_kh, kw = (n_dr + 1) // 2, (n_dc + 1) // 2
    assert 2 * GRID_W == LANES and kw <= GRID_W
    blocks, blk_idx = _nbr_plan(rows, min(n_kh, rows), n_kh)
    gp_blk0 = gp_col0 // hd
    cache_spec = pl.BlockSpec((1, 1, 1, past, hd), lambda h, b: (b, layer, h, 0, 0))
    return pl.pallas_call(
        functools.partial(_nbr_attn_kernel, blocks=blocks, blk_idx=blk_idx, hd=hd, kw=kw, n_dr=n_dr,
                          n_dc=n_dc),
        grid=(heads, nb),
        in_specs=[pl.BlockSpec(memory_space=pltpu.SMEM),
                  pl.BlockSpec((seq, hd), lambda h, b: (b, h)),
                  pl.BlockSpec((seq, hd), lambda h, b: (b, heads + h)),
                  pl.BlockSpec((seq, hd), lambda h, b: (b, 2 * heads + h)),
                  cache_spec, cache_spec,
                  pl.BlockSpec((seq, hd), lambda h, b: (b, gp_blk0 + h)),
                  pl.BlockSpec((1, hd), lambda h, b: (0, 0)),
                  pl.BlockSpec((1, hd), lambda h, b: (0, 0))],
        out_specs=pl.BlockSpec((seq, hd), lambda h, b: (b, h)),
        out_shape=jax.ShapeDtypeStruct((t, d_attn), BF16),
        scratch_shapes=[pltpu.VMEM((seq, hd), BF16), pltpu.VMEM((seq, hd), BF16),
                        pltpu.VMEM((seq, 2 * hd), BF16),
                        pltpu.VMEM((blk_idx.shape[0], NA_QROWS * GRID_W, NA_KROWS * GRID_W), F32)],
        compiler_params=_cparams("arbitrary", "arbitrary"),
    )(rel_bias.reshape(-1), u, u, u, cache_k, cache_v, u, qg.reshape(1, hd), kg.reshape(1, hd))


def _conv_kernel(a_ref, b_ref, ap_ref, bp_ref, an_ref, bn_ref, gp_ref, dw_ref, dwb_ref, lng_ref, lnb_ref,
                 wpw_ref, o_ref, h_scr, c_scr, z_scr, *, tiles_per_seq, width):
    tm, dc = a_ref.shape
    nc = dc // LANES
    pos = pl.program_id(0) % tiles_per_seq
    keep_prev = (pos > 0).astype(F32)
    keep_next = (pos < tiles_per_seq - 1).astype(F32)
    for c in range(nc):
        cols = slice(c * LANES, (c + 1) * LANES)
        h_scr[c, 0:CONV_HALO, :] = keep_prev * (
            ap_ref[:, cols].astype(F32) * _sigmoid(bp_ref[:, cols].astype(F32)))
        h_scr[c, CONV_HALO:CONV_HALO + tm, :] = (
            a_ref[:, cols].astype(F32) * _sigmoid(b_ref[:, cols].astype(F32)))
        h_scr[c, CONV_HALO + tm:, :] = keep_next * (
            an_ref[:, cols].astype(F32) * _sigmoid(bn_ref[:, cols].astype(F32)))

    off = CONV_HALO - width // 2

    def chunk(c, carry):
        for rb in range(tm // CONV_ROWS):
            acc = jnp.zeros((CONV_ROWS, LANES), F32)
            for k in range(width):
                acc = acc + dw_ref[c, k:k + 1, :] * h_scr[c, pl.ds(rb * CONV_ROWS + off + k, CONV_ROWS), :]
            c_scr[c, rb * CONV_ROWS:(rb + 1) * CONV_ROWS, :] = acc + dwb_ref[c]
        return carry

    lax.fori_loop(0, nc, chunk, 0)

    s1 = jnp.zeros((tm, LANES), F32)
    for c in range(nc):
        s1 = s1 + c_scr[c]
    mu = jnp.sum(s1, axis=-1, keepdims=True) / dc
    s2 = jnp.zeros((tm, LANES), F32)
    for c in range(nc):
        xc = c_scr[c] - mu
        s2 = s2 + xc * xc
    rstd = lax.rsqrt(jnp.sum(s2, axis=-1, keepdims=True) / dc + EPS)
    for c in range(nc):
        cols = slice(c * LANES, (c + 1) * LANES)
        y = (c_scr[c] - mu) * rstd * lng_ref[:, cols] + lnb_ref[:, cols]
        z_scr[:, cols] = _silu(y).astype(BF16)
    out = jnp.dot(z_scr[...], wpw_ref[...], preferred_element_type=F32)
    o_ref[...] = (out * _silu(gp_ref[...].astype(F32))).astype(o_ref.dtype)


def _conv_module(u, dw, dw_b, ln_g, ln_b, w_pw, seq, in_col0, gp_col0):
    width, dc = dw.shape
    t = u.shape[0]
    tm = CONV_TILE
    assert seq % tm == 0 and width // 2 <= CONV_HALO and tm % CONV_HALO == 0 and dc % LANES == 0
    nc = dc // LANES
    tiles_per_seq = seq // tm
    hb = tm // CONV_HALO
    n_hblk = t // CONV_HALO
    a_blk, gp_blk = in_col0 // dc, gp_col0 // dc
    wpad = -width % 8
    dw3 = jnp.pad(dw, ((0, wpad), (0, 0))).reshape(width + wpad, nc, LANES).transpose(1, 0, 2)
    prev = lambda i: jnp.maximum(i * hb - 1, 0)
    nxt = lambda i: jnp.minimum((i + 1) * hb, n_hblk - 1)
    vec = lambda x: x.reshape(1, dc)
    return pl.pallas_call(
        functools.partial(_conv_kernel, tiles_per_seq=tiles_per_seq, width=width),
        grid=(t // tm,),
        in_specs=[pl.BlockSpec((tm, dc), lambda i: (i, a_blk)),
                  pl.BlockSpec((tm, dc), lambda i: (i, a_blk + 1)),
                  pl.BlockSpec((CONV_HALO, dc), lambda i: (prev(i), a_blk)),
                  pl.BlockSpec((CONV_HALO, dc), lambda i: (prev(i), a_blk + 1)),
                  pl.BlockSpec((CONV_HALO, dc), lambda i: (nxt(i), a_blk)),
                  pl.BlockSpec((CONV_HALO, dc), lambda i: (nxt(i), a_blk + 1)),
                  pl.BlockSpec((tm, dc), lambda i: (i, gp_blk)),
                  pl.BlockSpec((nc, width + wpad, LANES), lambda i: (0, 0, 0)),
                  pl.BlockSpec((nc, 1, LANES), lambda i: (0, 0, 0)),
                  pl.BlockSpec((1, dc), lambda i: (0, 0)),
                  pl.BlockSpec((1, dc), lambda i: (0, 0)),
                  pl.BlockSpec((dc, dc), lambda i: (0, 0))],
        out_specs=pl.BlockSpec((tm, dc), lambda i: (i, 0)),
        out_shape=jax.ShapeDtypeStruct((t, dc), BF16),
        scratch_shapes=[pltpu.VMEM((nc, tm + 2 * CONV_HALO, LANES), F32),
                        pltpu.VMEM((nc, tm, LANES), F32),
                        pltpu.VMEM((tm, dc), BF16)],
        compiler_params=_cparams("parallel"),
    )(u, u, u, u, u, u, u, dw3, dw_b.reshape(nc, 1, LANES), vec(ln_g), vec(ln_b), w_pw)


def _with_casts(fn, wanted, wts, bf16):
    todo = [key for key in wanted if key not in bf16]
    out, copies = fn(side=[wts[name][jj] for name, jj in todo])
    bf16.update(zip(todo, copies))
    return out


def _in_proj(h, i, wts, bf16):
    j = i // 2
    own = ("w_in_even", j) if i % 2 == 0 else ("w_in_odd", j)
    wanted = [("w_out_even", j)] if i % 2 == 0 else [("w_out_odd", j), ("w_conv_pw", j)]
    if i + 1 < wts["norm_g"].shape[0]:
        wanted.append(("w_in_odd" if i % 2 == 0 else "w_in_even", (i + 1) // 2))
    if own not in bf16:
        bf16[own] = wts[own[0]][own[1]].astype(BF16)
    return _with_casts(functools.partial(_matmul, h, bf16[own]), wanted, wts, bf16)


def _layer_out(y1, y2, x, mod3, row_of, i, wts, bf16):
    own = ("w_out_even", i // 2) if i % 2 == 0 else ("w_out_odd", i // 2)
    return _with_casts(functools.partial(_out_proj, y1, y2, bf16[own], x, mod3, row_of), [], wts, bf16)


def _trunk(x, nb, seq, row_of, mod, caches, wts, bf16):
    d = x.shape[1]
    depth = mod.shape[0]
    new_k, new_v = [], []
    for i in range(depth):
        mod3 = mod[i][:, None, :]
        h = _norm_mod(x, wts["norm_g"][i], mod3, row_of)
        j = i // 2
        if i % 2 == 0:
            u = _in_proj(h, i, wts, bf16)
            d_f = wts["w_fourier"].shape[1] * wts["w_fourier"].shape[2]
            d_inner = d_f + wts["w_pool"].shape[1] * wts["w_pool"].shape[2]
            y1 = _fourier(u, wts["dft_w"][j], nb, seq, gp_col0=d_inner)
            y2 = _pool(u, wts["w_pool"][j], wts["pool_scale"][j], nb, seq, in_col0=d_f,
                       gp_col0=d_inner + d_f)
            x = _layer_out(y1, y2, x, mod3, row_of, i, wts, bf16)
        else:
            u = _in_proj(h, i, wts, bf16)
            d_conv = wts["conv_dw"].shape[2]
            d_attn = (u.shape[1] - 3 * d_conv) // 4
            gp_col0 = 3 * d_attn + 2 * d_conv
            if caches is None:
                y1, sk, sv = _ctx_attn(u, wts["q_norm_g"][j], wts["k_norm_g"][j], nb, seq, d_attn, gp_col0)
                new_k.append(sk)
                new_v.append(sv)
            else:
                y1 = _nbr_attn(u, caches[0], caches[1], j, wts["rel_bias"][j], wts["q_norm_g"][j],
                               wts["k_norm_g"][j], nb, seq, d_attn, gp_col0)
            y2 = _conv_module(u, wts["conv_dw"][j], wts["conv_dw_b"][j], wts["conv_ln_g"][j],
                              wts["conv_ln_b"][j], bf16["w_conv_pw", j], seq, in_col0=3 * d_attn,
                              gp_col0=gp_col0 + d_attn)
            x = _layer_out(y1, y2, x, mod3, row_of, i, wts, bf16)
    return x, new_k, new_v


def kernel(x_prompt, x_sample, cache_k, cache_v, c, c_ctx, norm_g, w_ada, b_ada, w_in_even, w_out_even,
           w_fourier, w_pool, pool_scale, w_in_odd, w_out_odd, q_norm_g, k_norm_g, rel_bias, conv_dw,
           conv_dw_b, conv_ln_g, conv_ln_b, w_conv_pw):
    nb_p, seq_p, d = x_prompt.shape
    nb_s, seq_s, _ = x_sample.shape
    n_rows = -(-(1 + nb_s) // 8) * 8
    cond = jnp.concatenate([c_ctx[None, :], c, jnp.zeros((n_rows - 1 - nb_s, d), F32)], axis=0)
    mod = _adaln(cond, w_ada, b_ada)
    wts = dict(norm_g=norm_g, pool_scale=pool_scale, q_norm_g=q_norm_g, k_norm_g=k_norm_g,
               rel_bias=rel_bias, conv_dw=conv_dw, conv_dw_b=conv_dw_b, conv_ln_g=conv_ln_g,
               conv_ln_b=conv_ln_b, w_in_even=w_in_even, w_out_even=w_out_even, w_in_odd=w_in_odd,
               w_out_odd=w_out_odd, w_conv_pw=w_conv_pw, w_fourier=w_fourier,
               w_pool=w_pool.astype(BF16))
    wts["dft_w"] = [_dft_fold(w_fourier[j]) for j in range(w_fourier.shape[0])]
    bf16 = {}
    y_p, ks, vs = _trunk(x_prompt.reshape(nb_p * seq_p, d), nb_p, seq_p, lambda r: 0, mod, None, wts, bf16)
    y_s, _, _ = _trunk(x_sample.reshape(nb_s * seq_s, d), nb_s, seq_s, lambda r: 1 + r // seq_s, mod,
                       (cache_k, cache_v), wts, bf16)
    state_k = jnp.concatenate(ks, axis=1)
    state_v = jnp.concatenate(vs, axis=1)
    return (y_p.reshape(nb_p, seq_p, d), y_s.reshape(nb_s, seq_s, d), state_k, state_v)
```

```python
import functools

import numpy as np
import jax
import jax.numpy as jnp
from jax import lax
from jax.experimental import pallas as pl
from jax.experimental.pallas import tpu as pltpu

F32 = jnp.float32
BF16 = jnp.bfloat16

GRID_W = 64
POOL_WINDOWS = (2, 4, 8, 16)
EPS = 1e-6
NEG_INF = -1e30
LOG2E = float(np.log2(np.e))

LANES = 128
V7X_VMEM_LIMIT_BYTES = 60 * 1024 * 1024

NORM_ROWS = 16

NA_QROWS = 4
NA_KROWS = 12
POOL_TILE = 256
POOL_HALO = 128
POOL_BLOCK_ROWS = 2048
FOURIER_BLOCK_ROWS = 1024
FOURIER_REV_ROWS = 256
FOURIER_FOLD_MIN_SEQ = 1024
CONV_TILE = 256
CONV_HALO = 16
CONV_ROWS = 64


def _cparams(*sem):
    return pltpu.CompilerParams(dimension_semantics=sem, vmem_limit_bytes=V7X_VMEM_LIMIT_BYTES)


def _pick(n, pref, unit=LANES):
    if n <= pref:
        return n
    t = (pref // unit) * unit
    while t > unit and n % t:
        t -= unit
    assert n % t == 0, (n, pref, unit)
    return t


def _sigmoid(x):
    return 0.5 * jnp.tanh(0.5 * x) + 0.5


def _silu(x):
    return x * _sigmoid(x)


def _rms(x, g):
    return x * lax.rsqrt(jnp.mean(x * x, axis=-1, keepdims=True) + EPS) * g


def _adaln_kernel(cond_ref, w_ref, b_ref, o_ref):
    s = _silu(cond_ref[...]).astype(BF16)
    o_ref[0] = jnp.dot(s, w_ref[0].astype(BF16), preferred_element_type=F32) + b_ref[0]


def _adaln(cond, w_ada, b_ada):
    depth, d, n = w_ada.shape
    r = cond.shape[0]
    tn = _pick(n, 1024)
    return pl.pallas_call(
        _adaln_kernel,
        grid=(depth, n // tn),
        in_specs=[pl.BlockSpec((r, d), lambda l, j: (0, 0)),
                  pl.BlockSpec((1, d, tn), lambda l, j: (l, 0, j)),
                  pl.BlockSpec((1, 1, tn), lambda l, j: (l, 0, j))],
        out_specs=pl.BlockSpec((1, r, tn), lambda l, j: (l, 0, j)),
        out_shape=jax.ShapeDtypeStruct((depth, r, n), F32),
        compiler_params=_cparams("parallel", "parallel"),
    )(cond, w_ada, b_ada.reshape(depth, 1, n))


def _norm_mod_kernel(x_ref, g_ref, shift_ref, scale_ref, o_ref, gain_scr, shift_scr):
    gain_scr[...] = jnp.broadcast_to(g_ref[...] * (1.0 + scale_ref[0]), gain_scr.shape)
    shift_scr[...] = jnp.broadcast_to(shift_ref[0], shift_scr.shape)

    def rows16(r, carry):
        rows = pl.ds(pl.multiple_of(r * NORM_ROWS, NORM_ROWS), NORM_ROWS)
        x = x_ref[rows, :]
        rstd = lax.rsqrt(jnp.mean(x * x, axis=-1, keepdims=True) + EPS)
        o_ref[rows, :] = (x * rstd * gain_scr[...] + shift_scr[...]).astype(o_ref.dtype)
        return carry

    lax.fori_loop(0, x_ref.shape[0] // NORM_ROWS, rows16, 0, unroll=4)


def _norm_mod(x, g, mod3, row_of):
    t, d = x.shape
    tm = _pick(t, 1024, 8)
    return pl.pallas_call(
        _norm_mod_kernel,
        grid=(t // tm,),
        in_specs=[pl.BlockSpec((tm, d), lambda i: (i, 0)),
                  pl.BlockSpec((1, d), lambda i: (0, 0)),
                  pl.BlockSpec((1, 1, d), lambda i: (row_of(i * tm), 0, 0)),
                  pl.BlockSpec((1, 1, d), lambda i: (row_of(i * tm), 0, 1))],
        out_specs=pl.BlockSpec((tm, d), lambda i: (i, 0)),
        out_shape=jax.ShapeDtypeStruct((t, d), BF16),
        scratch_shapes=[pltpu.VMEM((NORM_ROWS, d), F32), pltpu.VMEM((NORM_ROWS, d), F32)],
        compiler_params=_cparams("parallel"),
    )(x, g.reshape(1, d), mod3, mod3)


def _side_casts(side, gi, gj):
    specs, shapes = [], []
    for s in side:
        rows, cols = s.shape
        nblk = min(gi * gj, rows // 16)
        while rows % nblk or (rows // nblk) % 16:
            nblk -= 1
        specs.append(pl.BlockSpec((rows // nblk, cols),
                                  lambda i, j, nblk=nblk: (jnp.minimum(i * gj + j, nblk - 1), 0)))
        shapes.append(jax.ShapeDtypeStruct(s.shape, BF16))
    return specs, shapes


def _cast_side(side_refs):
    n = len(side_refs) // 2
    for src_ref, dst_ref in zip(side_refs[:n], side_refs[n:]):
        dst_ref[...] = src_ref[...].astype(dst_ref.dtype)


def _matmul_kernel(a_ref, w_ref, *refs):
    n_side = (len(refs) - 1) // 2
    o_ref = refs[n_side]
    o_ref[...] = jnp.dot(a_ref[...], w_ref[...], preferred_element_type=F32).astype(o_ref.dtype)
    _cast_side(refs[:n_side] + refs[n_side + 1:])


def _matmul(a, w, side=()):
    m, k = a.shape
    n = w.shape[1]
    bm, bn = _pick(m, 1024, 8), _pick(n, 1024)
    gi, gj = m // bm, n // bn
    side_specs, side_shapes = _side_casts(side, gi, gj)
    outs = pl.pallas_call(
        _matmul_kernel,
        grid=(gi, gj),
        in_specs=[pl.BlockSpec((bm, k), lambda i, j: (i, 0)),
                  pl.BlockSpec((k, bn), lambda i, j: (0, j))] + side_specs,
        out_specs=[pl.BlockSpec((bm, bn), lambda i, j: (i, j))] + side_specs,
        out_shape=[jax.ShapeDtypeStruct((m, n), BF16)] + side_shapes,
        compiler_params=_cparams("arbitrary", "arbitrary"),
    )(a, w, *side)
    return outs[0], list(outs[1:])


def _out_proj_kernel(a1_ref, a2_ref, w1_ref, w2_ref, x_ref, gate_ref, *refs):
    n_side = (len(refs) - 1) // 2
    o_ref = refs[n_side]
    acc = jnp.dot(a1_ref[...], w1_ref[...], preferred_element_type=F32)
    acc = acc + jnp.dot(a2_ref[...], w2_ref[...], preferred_element_type=F32)
    o_ref[...] = x_ref[...] + gate_ref[0] * acc
    _cast_side(refs[:n_side] + refs[n_side + 1:])


def _out_proj(a1, a2, w, x, mod3, row_of, side=()):
    t, k1 = a1.shape
    d = w.shape[1]
    assert a2.shape == (t, k1) and w.shape[0] == 2 * k1
    bm, bn = _pick(t, 1024, 8), _pick(d, 512 if side else 1024)
    gi, gj = t // bm, d // bn
    gate_blk = 2 * gj
    side_specs, side_shapes = _side_casts(side, gi, gj)
    outs = pl.pallas_call(
        _out_proj_kernel,
        grid=(gi, gj),
        in_specs=[pl.BlockSpec((bm, k1), lambda i, j: (i, 0)),
                  pl.BlockSpec((bm, k1), lambda i, j: (i, 0)),
                  pl.BlockSpec((k1, bn), lambda i, j: (0, j)),
                  pl.BlockSpec((k1, bn), lambda i, j: (1, j)),
                  pl.BlockSpec((bm, bn), lambda i, j: (i, j)),
                  pl.BlockSpec((1, 1, bn), lambda i, j: (row_of(i * bm), 0, gate_blk + j))] + side_specs,
        out_specs=[pl.BlockSpec((bm, bn), lambda i, j: (i, j))] + side_specs,
        out_shape=[jax.ShapeDtypeStruct((t, d), F32)] + side_shapes,
        compiler_params=_cparams("arbitrary", "arbitrary"),
    )(a1, a2, w, w, x, mod3, *side)
    return outs[0], list(outs[1:])


def _dft_cos_sin(n):
    idx = np.arange(n, dtype=np.int64)
    ang = 2.0 * np.pi * ((idx[:, None] * idx[None, :]) % n) / n
    return np.cos(ang) / np.sqrt(n), np.sin(ang) / np.sqrt(n)


def _dft_fold_kernel(c_ref, s_ref, wf_ref, o_ref):
    cg = wf_ref.shape[1]
    hi = lax.Precision.HIGHEST
    o_ref[0, :, 0:cg] = jnp.dot(c_ref[...], wf_ref[0], precision=hi, preferred_element_type=F32).astype(BF16)
    o_ref[0, :, cg:] = jnp.dot(s_ref[...], wf_ref[0], precision=hi, preferred_element_type=F32).astype(BF16)


def _dft_fold(wf):
    groups, cg, _ = wf.shape
    cc, sc = _dft_cos_sin(cg)
    const = pl.BlockSpec((cg, cg), lambda g: (0, 0))
    return pl.pallas_call(
        _dft_fold_kernel,
        grid=(groups,),
        in_specs=[const, const, pl.BlockSpec((1, cg, cg), lambda g: (g, 0, 0))],
        out_specs=pl.BlockSpec((1, cg, 2 * cg), lambda g: (g, 0, 0)),
        out_shape=jax.ShapeDtypeStruct((groups, cg, 2 * cg), BF16),
        compiler_params=_cparams("parallel"),
    )(jnp.asarray(cc, F32), jnp.asarray(sc, F32), wf)


def _fourier_kernel(u_ref, gp_ref, cw_ref, csl_ref, jsh_ref, o_ref, q_scr, mid_scr, *, seq, cg, fold):
    if not fold:
        @pl.when(pl.program_id(2) == 0)
        def _():
            for s in range(u_ref.shape[0] // seq):
                p = jnp.dot(u_ref[s * seq:(s + 1) * seq, :], cw_ref[0], preferred_element_type=F32)
                q_scr[s, 0:seq, :] = p[:, :cg].astype(BF16)
                q_scr[s, seq:2 * seq, :] = p[:, cg:].astype(BF16)

        for s in range(u_ref.shape[0] // seq):
            rows = slice(s * csl_ref.shape[0], (s + 1) * csl_ref.shape[0])
            y = jnp.dot(csl_ref[...], q_scr[s], preferred_element_type=F32)
            o_ref[rows, :] = (y * _silu(gp_ref[rows, :].astype(F32))).astype(o_ref.dtype)
        return

    half = seq // 2
    rb = jsh_ref.shape[0]
    nblk = half // rb
    n_seq = u_ref.shape[0] // seq
    tr = csl_ref.shape[0]

    @pl.when(pl.program_id(2) == 0)
    def _():
        row0 = lax.broadcasted_iota(jnp.int32, (rb, 2 * cg), 0) == 0
        sign = jnp.where(lax.broadcasted_iota(jnp.int32, (1, 2 * cg), 1) < cg, 1.0, -1.0)
        for s in range(n_seq):
            p = jnp.dot(u_ref[s * seq:(s + 1) * seq, :], cw_ref[0], preferred_element_type=F32)
            mid_scr[s] = p[half:half + 8, :cg]
            upper = p[half:, :].astype(BF16)
            for a in range(nblk):
                rev = jnp.dot(jsh_ref[...], upper[(nblk - 1 - a) * rb:(nblk - a) * rb, :],
                              preferred_element_type=F32)
                if a > 0:
                    first = half + (nblk - a) * rb
                    rev = jnp.where(row0, p[first:first + 1, :], rev)
                q = p[a * rb:(a + 1) * rb, :] + sign * rev
                q_scr[s, a * rb:(a + 1) * rb, :] = q[:, :cg].astype(BF16)
                q_scr[s, half + a * rb:half + (a + 1) * rb, :] = q[:, cg:].astype(BF16)

    odd_row = (lax.broadcasted_iota(jnp.int32, (tr, 1), 0) & 1) == 1
    cos_mid = jnp.where(odd_row, -1.0, 1.0) * seq ** -0.5
    for s in range(n_seq):
        rows = slice(s * tr, (s + 1) * tr)
        y = jnp.dot(csl_ref[...], q_scr[s], preferred_element_type=F32) + cos_mid * mid_scr[s, 0:1, :]
        o_ref[rows, :] = (y * _silu(gp_ref[rows, :].astype(F32))).astype(o_ref.dtype)


def _fourier(u, cw, nb, seq, gp_col0):
    groups, cg, _ = cw.shape
    t = u.shape[0]
    tr = _pick(seq, 512, 8)
    rt = seq // tr
    spb = 1
    if rt == 1:
        spb = max(1, min(nb, FOURIER_BLOCK_ROWS // seq))
        while nb % spb:
            spb -= 1
    fold = seq >= FOURIER_FOLD_MIN_SEQ
    half = seq // 2
    assert tr % 2 == 0 and half % 16 == 0
    cl, sl = _dft_cos_sin(seq)
    kpos = half if fold else seq
    csl = jnp.asarray(np.concatenate([cl[:, :kpos], -sl[:, :kpos]], axis=1), BF16)
    rb = _pick(half, FOURIER_REV_ROWS, 16)
    jsh = np.zeros((rb, rb), np.float32)
    jsh[np.arange(1, rb), rb - np.arange(1, rb)] = 1.0
    gp_blk0 = gp_col0 // cg
    return pl.pallas_call(
        functools.partial(_fourier_kernel, seq=seq, cg=cg, fold=fold),
        grid=(nb // spb, groups, rt),
        in_specs=[pl.BlockSpec((spb * seq, cg), lambda b, g, r: (b, g)),
                  pl.BlockSpec((spb * tr, cg), lambda b, g, r: (b * rt + r, gp_blk0 + g)),
                  pl.BlockSpec((1, cg, 2 * cg), lambda b, g, r: (g, 0, 0)),
                  pl.BlockSpec((tr, 2 * kpos), lambda b, g, r: (r, 0)),
                  pl.BlockSpec((rb, rb), lambda b, g, r: (0, 0))],
        out_specs=pl.BlockSpec((spb * tr, cg), lambda b, g, r: (b * rt + r, g)),
        out_shape=jax.ShapeDtypeStruct((t, groups * cg), BF16),
        scratch_shapes=[pltpu.VMEM((spb, 2 * kpos, cg), BF16), pltpu.VMEM((spb, 8, cg), F32)],
        compiler_params=_cparams("parallel", "parallel", "arbitrary"),
    )(u, u, cw, csl, jnp.asarray(jsh, BF16))


def _pool_tables(seq):
    r = np.arange(POOL_TILE)[:, None]
    s = np.arange(POOL_TILE + 2 * POOL_HALO)[None, :] - POOL_HALO
    tpos = np.arange(seq)
    bands, cnts = [], []
    for win in POOL_WINDOWS:
        half = win // 2
        assert half <= POOL_HALO
        bands.append(((s >= r - half) & (s <= r + half - 1)).astype(np.float32))
        cnts.append(np.minimum(tpos + half, seq) - np.maximum(tpos - half, 0))
    return np.stack(bands), np.stack(cnts).astype(np.float32)[:, :, None]


def _pool_kernel(u_ref, gp_ref, band_ref, cnt_ref, w_ref, ps_ref, o_ref, *, seq):
    tr = POOL_TILE
    for s0 in range(0, u_ref.shape[0], seq):
        for t in range(seq // tr):
            base = tr * t - POOL_HALO
            lo, hi = max(base, 0), min(base + tr + 2 * POOL_HALO, seq)
            rows = slice(s0 + tr * t, s0 + tr * (t + 1))
            wsum = jnp.dot(band_ref[0, :, lo - base:hi - base], u_ref[s0 + lo:s0 + hi, :],
                           preferred_element_type=F32)
            cnt = cnt_ref[0, tr * t:tr * (t + 1), :]
            d = (wsum / cnt - u_ref[rows, :].astype(F32)).astype(BF16)
            y = jnp.dot(d, w_ref[0], preferred_element_type=F32) * ps_ref[...]
            o_ref[rows, :] = (y * _silu(gp_ref[rows, :].astype(F32))).astype(o_ref.dtype)


def _pool(u, wp, pool_scale, nb, seq, in_col0, gp_col0):
    groups, cg, _ = wp.shape
    assert groups == len(POOL_WINDOWS) and seq % POOL_TILE == 0
    t = u.shape[0]
    band, cnt = _pool_tables(seq)
    in_blk0, gp_blk0 = in_col0 // cg, gp_col0 // cg
    spb = max(1, min(nb, POOL_BLOCK_ROWS // seq))
    while nb % spb:
        spb -= 1
    rows = spb * seq
    return pl.pallas_call(
        functools.partial(_pool_kernel, seq=seq),
        grid=(nb // spb, groups),
        in_specs=[pl.BlockSpec((rows, cg), lambda b, g: (b, in_blk0 + g)),
                  pl.BlockSpec((rows, cg), lambda b, g: (b, gp_blk0 + g)),
                  pl.BlockSpec((1,) + band.shape[1:], lambda b, g: (g, 0, 0)),
                  pl.BlockSpec((1, seq, 1), lambda b, g: (g, 0, 0)),
                  pl.BlockSpec((1, cg, cg), lambda b, g: (g, 0, 0)),
                  pl.BlockSpec((1, cg), lambda b, g: (0, g))],
        out_specs=pl.BlockSpec((rows, cg), lambda b, g: (b, g)),
        out_shape=jax.ShapeDtypeStruct((t, groups * cg), BF16),
        compiler_params=_cparams("parallel", "parallel"),
    )(u, u, jnp.asarray(band, BF16), jnp.asarray(cnt, F32), wp, pool_scale.reshape(1, groups * cg))


def _ctx_attn_kernel(q_ref, k_ref, v_ref, gp_ref, qg_ref, kg_ref, o_ref, sk_ref, sv_ref, *, heads, hd):
    qscale = hd ** -0.5 * LOG2E
    for h in range(heads):
        cols = slice(h * hd, (h + 1) * hd)
        qn = _rms(q_ref[:, cols].astype(F32), qg_ref[...]) * qscale
        kn = _rms(k_ref[:, cols].astype(F32), kg_ref[...])
        v = v_ref[:, cols]
        sk_ref[0, 0, h] = kn
        sv_ref[0, 0, h] = v.astype(F32)
        s = lax.dot_general(qn.astype(BF16), kn.astype(BF16), (((1,), (1,)), ((), ())),
                            preferred_element_type=F32)
        p = jnp.exp2(s - jnp.max(s, axis=-1, keepdims=True))
        v_ones = jnp.concatenate([v, jnp.ones_like(v)], axis=1)
        o = jnp.dot(p.astype(BF16), v_ones, preferred_element_type=F32)
        o = o[:, :hd] / o[:, hd:]
        o_ref[:, cols] = (o * _silu(gp_ref[:, cols].astype(F32))).astype(o_ref.dtype)


def _ctx_attn(u, qg, kg, nb, seq, d_attn, gp_col0):
    hd = qg.shape[-1]
    heads = d_attn // hd
    t = u.shape[0]
    state = jax.ShapeDtypeStruct((nb, 1, heads, seq, hd), F32)
    state_spec = pl.BlockSpec((1, 1, heads, seq, hd), lambda b: (b, 0, 0, 0, 0))
    gp_blk = gp_col0 // d_attn
    return pl.pallas_call(
        functools.partial(_ctx_attn_kernel, heads=heads, hd=hd),
        grid=(nb,),
        in_specs=[pl.BlockSpec((seq, d_attn), lambda b: (b, 0)),
                  pl.BlockSpec((seq, d_attn), lambda b: (b, 1)),
                  pl.BlockSpec((seq, d_attn), lambda b: (b, 2)),
                  pl.BlockSpec((seq, d_attn), lambda b: (b, gp_blk)),
                  pl.BlockSpec((1, hd), lambda b: (0, 0)),
                  pl.BlockSpec((1, hd), lambda b: (0, 0))],
        out_specs=[pl.BlockSpec((seq, d_attn), lambda b: (b, 0)), state_spec, state_spec],
        out_shape=[jax.ShapeDtypeStruct((t, d_attn), BF16), state, state],
        compiler_params=_cparams("parallel"),
    )(u, u, u, u, qg.reshape(1, hd), kg.reshape(1, hd))


def _nbr_plan(rows, kh, n_kh):
    assert rows % NA_QROWS == 0 and rows >= NA_KROWS
    n_dr = 2 * n_kh - 1
    blocks, types = [], []
    for r0 in range(0, rows, NA_QROWS):
        ws = int(np.clip(r0 - kh // 2, 0, rows - NA_KROWS))
        idx = np.full((NA_QROWS, NA_KROWS), n_dr, np.int32)
        for i in range(NA_QROWS):
            r = r0 + i
            start = int(np.clip(r - kh // 2, 0, rows - kh))
            assert ws <= start and start + kh <= ws + NA_KROWS
            for j in range(NA_KROWS):
                rk = ws + j
                if start <= rk < start + kh:
                    idx[i, j] = rk - r + n_kh - 1
        for t, known in enumerate(types):
            if np.array_equal(known, idx):
                break
        else:
            t = len(types)
            types.append(idx)
        blocks.append((r0, ws, t))
    return tuple(blocks), np.stack(types)


def _nbr_build_bias(rb_ref, bias_scr, head, blk_idx, n_dr, n_dc, kw):
    w = GRID_W
    c = lax.broadcasted_iota(jnp.int32, (w, 2 * w), 0)
    lane = lax.broadcasted_iota(jnp.int32, (w, 2 * w), 1)
    cp = lane % w
    q_start = jnp.clip(c - kw // 2, 0, w - kw)
    diff = jnp.where(cp >= q_start, jnp.where(cp < q_start + kw, cp - c + (kw - 1), -1), -1)
    neg = jnp.full((w, 2 * w), NEG_INF, F32)
    pairs = []
    for dr in range(n_dr):
        blk = neg
        for d in range(n_dc):
            blk = jnp.where(diff == d, rb_ref[(head * n_dr + dr) * n_dc + d] * LOG2E, blk)
        pairs.append(blk)
    pairs.append(neg)
    low = lane < w
    nt, qr, kr = blk_idx.shape
    for t in range(nt):
        for i in range(qr):
            for j in range(0, kr, 2):
                e, o = int(blk_idx[t, i, j]), int(blk_idx[t, i, j + 1])
                blk = pairs[e] if e == o else jnp.where(low, pairs[e], pairs[o])
                bias_scr[t, i * w:(i + 1) * w, j * w:(j + 2) * w] = blk


def _nbr_attn_kernel(rb_ref, q_ref, k_ref, v_ref, ck_ref, cv_ref, gp_ref, qg_ref, kg_ref, o_ref,
                     qn_scr, kn_scr, v1_scr, bias_scr, *, blocks, blk_idx, hd, kw, n_dr, n_dc):
    @pl.when(pl.program_id(1) == 0)
    def _():
        _nbr_build_bias(rb_ref, bias_scr, pl.program_id(0), blk_idx, n_dr, n_dc, kw)

    qscale = hd ** -0.5 * LOG2E
    qb, kb = NA_QROWS * GRID_W, NA_KROWS * GRID_W
    nt_dims = (((1,), (1,)), ((), ()))
    qn_scr[...] = (_rms(q_ref[...].astype(F32), qg_ref[...]) * qscale).astype(BF16)
    kn_scr[...] = _rms(k_ref[...].astype(F32), kg_ref[...]).astype(BF16)
    ck = ck_ref[0, 0, 0].astype(BF16)
    cv = cv_ref[0, 0, 0].astype(BF16)
    cv1 = jnp.concatenate([cv, jnp.ones_like(cv)], axis=1)
    v1_scr[:, 0:hd] = v_ref[...]
    v1_scr[:, hd:] = jnp.ones(v_ref.shape, BF16)

    def logits(n):
        r0, ws, t = blocks[n]
        q = qn_scr[r0 * GRID_W:r0 * GRID_W + qb, :]
        s_loc = lax.dot_general(q, kn_scr[ws * GRID_W:ws * GRID_W + kb, :], nt_dims,
                                preferred_element_type=F32) + bias_scr[t]
        return s_loc, lax.dot_general(q, ck, nt_dims, preferred_element_type=F32)

    def softmax(s_loc, s_ctx):
        m = jnp.maximum(jnp.max(s_loc, axis=-1, keepdims=True), jnp.max(s_ctx, axis=-1, keepdims=True))
        p_loc = jnp.exp2(s_loc - m)
        p_ctx = jnp.exp2(s_ctx - m)
        return p_loc.astype(BF16), p_ctx.astype(BF16)

    def attend(n, p_loc, p_ctx):
        r0, ws, _ = blocks[n]
        qs = slice(r0 * GRID_W, r0 * GRID_W + qb)
        o = jnp.dot(p_loc, v1_scr[ws * GRID_W:ws * GRID_W + kb, :], preferred_element_type=F32)
        o = o + jnp.dot(p_ctx, cv1, preferred_element_type=F32)
        o = o[:, :hd] / o[:, hd:]
        o_ref[qs, :] = (o * _silu(gp_ref[qs, :].astype(F32))).astype(o_ref.dtype)

    s_cur, p_prev = logits(0), None
    for n in range(len(blocks)):
        p_cur = softmax(*s_cur)
        if p_prev is not None:
            attend(n - 1, *p_prev)
        if n + 1 < len(blocks):
            s_cur = logits(n + 1)
        p_prev = p_cur
    attend(len(blocks) - 1, *p_prev)


def _nbr_attn(u, cache_k, cache_v, layer, rel_bias, qg, kg, nb, seq, d_attn, gp_col0):
    hd = qg.shape[-1]
    heads = d_attn // hd
    t = u.shape[0]
    past = cache_k.shape[3]
    rows = seq // GRID_W
    _, n_dr, n_dc = rel_bias.shape
    n_kh, kw = (n_dr + 1) // 2, (n_dc + 1) // 2
    assert 2 * GRID_W == LANES and kw <= GRID_W
    blocks, blk_idx = _nbr_plan(rows, min(n_kh, rows), n_kh)
    gp_blk0 = gp_col0 // hd
    cache_spec = pl.BlockSpec((1, 1, 1, past, hd), lambda h, b: (b, layer, h, 0, 0))
    return pl.pallas_call(
        functools.partial(_nbr_attn_kernel, blocks=blocks, blk_idx=blk_idx, hd=hd, kw=kw, n_dr=n_dr,
                          n_dc=n_dc),
        grid=(heads, nb),
        in_specs=[pl.BlockSpec(memory_space=pltpu.SMEM),
                  pl.BlockSpec((seq, hd), lambda h, b: (b, h)),
                  pl.BlockSpec((seq, hd), lambda h, b: (b, heads + h)),
                  pl.BlockSpec((seq, hd), lambda h, b: (b, 2 * heads + h)),
                  cache_spec, cache_spec,
                  pl.BlockSpec((seq, hd), lambda h, b: (b, gp_blk0 + h)),
                  pl.BlockSpec((1, hd), lambda h, b: (0, 0)),
                  pl.BlockSpec((1, hd), lambda h, b: (0, 0))],
        out_specs=pl.BlockSpec((seq, hd), lambda h, b: (b, h)),
        out_shape=jax.ShapeDtypeStruct((t, d_attn), BF16),
        scratch_shapes=[pltpu.VMEM((seq, hd), BF16), pltpu.VMEM((seq, hd), BF16),
                        pltpu.VMEM((seq, 2 * hd), BF16),
                        pltpu.VMEM((blk_idx.shape[0], NA_QROWS * GRID_W, NA_KROWS * GRID_W), F32)],
        compiler_params=_cparams("arbitrary", "arbitrary"),
    )(rel_bias.reshape(-1), u, u, u, cache_k, cache_v, u, qg.reshape(1, hd), kg.reshape(1, hd))


def _conv_kernel(a_ref, b_ref, ap_ref, bp_ref, an_ref, bn_ref, gp_ref, dw_ref, dwb_ref, lng_ref, lnb_ref,
                 wpw_ref, o_ref, h_scr, c_scr, z_scr, *, tiles_per_seq, width):
    tm, dc = a_ref.shape
    nc = dc // LANES
    pos = pl.program_id(0) % tiles_per_seq
    keep_prev = (pos > 0).astype(F32)
    keep_next = (pos < tiles_per_seq - 1).astype(F32)
    for c in range(nc):
        cols = slice(c * LANES, (c + 1) * LANES)
        h_scr[c, 0:CONV_HALO, :] = keep_prev * (
            ap_ref[:, cols].astype(F32) * _sigmoid(bp_ref[:, cols].astype(F32)))
        h_scr[c, CONV_HALO:CONV_HALO + tm, :] = (
            a_ref[:, cols].astype(F32) * _sigmoid(b_ref[:, cols].astype(F32)))
        h_scr[c, CONV_HALO + tm:, :] = keep_next * (
            an_ref[:, cols].astype(F32) * _sigmoid(bn_ref[:, cols].astype(F32)))

    off = CONV_HALO - width // 2

    def chunk(c, carry):
        for rb in range(tm // CONV_ROWS):
            acc = jnp.zeros((CONV_ROWS, LANES), F32)
            for k in range(width):
                acc = acc + dw_ref[c, k:k + 1, :] * h_scr[c, pl.ds(rb * CONV_ROWS + off + k, CONV_ROWS), :]
            c_scr[c, rb * CONV_ROWS:(rb + 1) * CONV_ROWS, :] = acc + dwb_ref[c]
        return carry

    lax.fori_loop(0, nc, chunk, 0)

    s1 = jnp.zeros((tm, LANES), F32)
    for c in range(nc):
        s1 = s1 + c_scr[c]
    mu = jnp.sum(s1, axis=-1, keepdims=True) / dc
    s2 = jnp.zeros((tm, LANES), F32)
    for c in range(nc):
        xc = c_scr[c] - mu
        s2 = s2 + xc * xc
    rstd = lax.rsqrt(jnp.sum(s2, axis=-1, keepdims=True) / dc + EPS)
    for c in range(nc):
        cols = slice(c * LANES, (c + 1) * LANES)
        y = (c_scr[c] - mu) * rstd * lng_ref[:, cols] + lnb_ref[:, cols]
        z_scr[:, cols] = _silu(y).astype(BF16)
    out = jnp.dot(z_scr[...], wpw_ref[...], preferred_element_type=F32)
    o_ref[...] = (out * _silu(gp_ref[...].astype(F32))).astype(o_ref.dtype)


def _conv_module(u, dw, dw_b, ln_g, ln_b, w_pw, seq, in_col0, gp_col0):
    width, dc = dw.shape
    t = u.shape[0]
    tm = CONV_TILE
    assert seq % tm == 0 and width // 2 <= CONV_HALO and tm % CONV_HALO == 0 and dc % LANES == 0
    nc = dc // LANES
    tiles_per_seq = seq // tm
    hb = tm // CONV_HALO
    n_hblk = t // CONV_HALO
    a_blk, gp_blk = in_col0 // dc, gp_col0 // dc
    wpad = -width % 8
    dw3 = jnp.pad(dw, ((0, wpad), (0, 0))).reshape(width + wpad, nc, LANES).transpose(1, 0, 2)
    prev = lambda i: jnp.maximum(i * hb - 1, 0)
    nxt = lambda i: jnp.minimum((i + 1) * hb, n_hblk - 1)
    vec = lambda x: x.reshape(1, dc)
    return pl.pallas_call(
        functools.partial(_conv_kernel, tiles_per_seq=tiles_per_seq, width=width),
        grid=(t // tm,),
        in_specs=[pl.BlockSpec((tm, dc), lambda i: (i, a_blk)),
                  pl.BlockSpec((tm, dc), lambda i: (i, a_blk + 1)),
                  pl.BlockSpec((CONV_HALO, dc), lambda i: (prev(i), a_blk)),
                  pl.BlockSpec((CONV_HALO, dc), lambda i: (prev(i), a_blk + 1)),
                  pl.BlockSpec((CONV_HALO, dc), lambda i: (nxt(i), a_blk)),
                  pl.BlockSpec((CONV_HALO, dc), lambda i: (nxt(i), a_blk + 1)),
                  pl.BlockSpec((tm, dc), lambda i: (i, gp_blk)),
                  pl.BlockSpec((nc, width + wpad, LANES), lambda i: (0, 0, 0)),
                  pl.BlockSpec((nc, 1, LANES), lambda i: (0, 0, 0)),
                  pl.BlockSpec((1, dc), lambda i: (0, 0)),
                  pl.BlockSpec((1, dc), lambda i: (0, 0)),
                  pl.BlockSpec((dc, dc), lambda i: (0, 0))],
        out_specs=pl.BlockSpec((tm, dc), lambda i: (i, 0)),
        out_shape=jax.ShapeDtypeStruct((t, dc), BF16),
        scratch_shapes=[pltpu.VMEM((nc, tm + 2 * CONV_HALO, LANES), F32),
                        pltpu.VMEM((nc, tm, LANES), F32),
                        pltpu.VMEM((tm, dc), BF16)],
        compiler_params=_cparams("parallel"),
    )(u, u, u, u, u, u, u, dw3, dw_b.reshape(nc, 1, LANES), vec(ln_g), vec(ln_b), w_pw)


def _with_casts(fn, wanted, wts, bf16):
    todo = [key for key in wanted if key not in bf16]
    out, copies = fn(side=[wts[name][jj] for name, jj in todo])
    bf16.update(zip(todo, copies))
    return out


def _in_proj(h, i, wts, bf16):
    j = i // 2
    own = ("w_in_even", j) if i % 2 == 0 else ("w_in_odd", j)
    wanted = [("w_out_even", j)] if i % 2 == 0 else [("w_out_odd", j), ("w_conv_pw", j)]
    if i + 1 < wts["norm_g"].shape[0]:
        wanted.append(("w_in_odd" if i % 2 == 0 else "w_in_even", (i + 1) // 2))
    if own not in bf16:
        bf16[own] = wts[own[0]][own[1]].astype(BF16)
    return _with_casts(functools.partial(_matmul, h, bf16[own]), wanted, wts, bf16)


def _layer_out(y1, y2, x, mod3, row_of, i, wts, bf16):
    own = ("w_out_even", i // 2) if i % 2 == 0 else ("w_out_odd", i // 2)
    return _with_casts(functools.partial(_out_proj, y1, y2, bf16[own], x, mod3, row_of), [], wts, bf16)


def _trunk(x, nb, seq, row_of, mod, caches, wts, bf16):
    d = x.shape[1]
    depth = mod.shape[0]
    new_k, new_v = [], []
    for i in range(depth):
        mod3 = mod[i][:, None, :]
        h = _norm_mod(x, wts["norm_g"][i], mod3, row_of)
        j = i // 2
        if i % 2 == 0:
            u = _in_proj(h, i, wts, bf16)
            d_f = wts["w_fourier"].shape[1] * wts["w_fourier"].shape[2]
            d_inner = d_f + wts["w_pool"].shape[1] * wts["w_pool"].shape[2]
            y1 = _fourier(u, wts["dft_w"][j], nb, seq, gp_col0=d_inner)
            y2 = _pool(u, wts["w_pool"][j], wts["pool_scale"][j], nb, seq, in_col0=d_f,
                       gp_col0=d_inner + d_f)
            x = _layer_out(y1, y2, x, mod3, row_of, i, wts, bf16)
        else:
            u = _in_proj(h, i, wts, bf16)
            d_conv = wts["conv_dw"].shape[2]
            d_attn = (u.shape[1] - 3 * d_conv) // 4
            gp_col0 = 3 * d_attn + 2 * d_conv
            if caches is None:
                y1, sk, sv = _ctx_attn(u, wts["q_norm_g"][j], wts["k_norm_g"][j], nb, seq, d_attn, gp_col0)
                new_k.append(sk)
                new_v.append(sv)
            else:
                y1 = _nbr_attn(u, caches[0], caches[1], j, wts["rel_bias"][j], wts["q_norm_g"][j],
                               wts["k_norm_g"][j], nb, seq, d_attn, gp_col0)
            y2 = _conv_module(u, wts["conv_dw"][j], wts["conv_dw_b"][j], wts["conv_ln_g"][j],
                              wts["conv_ln_b"][j], bf16["w_conv_pw", j], seq, in_col0=3 * d_attn,
                              gp_col0=gp_col0 + d_attn)
            x = _layer_out(y1, y2, x, mod3, row_of, i, wts, bf16)
    return x, new_k, new_v


def kernel(x_prompt, x_sample, cache_k, cache_v, c, c_ctx, norm_g, w_ada, b_ada, w_in_even, w_out_even,
           w_fourier, w_pool, pool_scale, w_in_odd, w_out_odd, q_norm_g, k_norm_g, rel_bias, conv_dw,
           conv_dw_b, conv_ln_g, conv_ln_b, w_conv_pw):
    nb_p, seq_p, d = x_prompt.shape
    nb_s, seq_s, _ = x_sample.shape
    n_rows = -(-(1 + nb_s) // 8) * 8
    cond = jnp.concatenate([c_ctx[None, :], c, jnp.zeros((n_rows - 1 - nb_s, d), F32)], axis=0)
    mod = _adaln(cond, w_ada, b_ada)
    wts = dict(norm_g=norm_g, pool_scale=pool_scale, q_norm_g=q_norm_g, k_norm_g=k_norm_g,
               rel_bias=rel_bias, conv_dw=conv_dw, conv_dw_b=conv_dw_b, conv_ln_g=conv_ln_g,
               conv_ln_b=conv_ln_b, w_in_even=w_in_even, w_out_even=w_out_even, w_in_odd=w_in_odd,
               w_out_odd=w_out_odd, w_conv_pw=w_conv_pw, w_fourier=w_fourier,
               w_pool=w_pool.astype(BF16))
    wts["dft_w"] = [_dft_fold(w_fourier[j]) for j in range(w_fourier.shape[0])]
    bf16 = {}
    y_p, ks, vs = _trunk(x_prompt.reshape(nb_p * seq_p, d), nb_p, seq_p, lambda r: 0, mod, None, wts, bf16)
    y_s, _, _ = _trunk(x_sample.reshape(nb_s * seq_s, d), nb_s, seq_s, lambda r: 1 + r // seq_s, mod,
                       (cache_k, cache_v), wts, bf16)
    state_k = jnp.concatenate(ks, axis=1)
    state_v = jnp.concatenate(vs, axis=1)
    return (y_p.reshape(nb_p, seq_p, d), y_s.reshape(nb_s, seq_s, d), state_k, state_v)
```

```python
import functools

import numpy as np
import jax
import jax.numpy as jnp
from jax import lax
from jax.experimental import pallas as pl
from jax.experimental.pallas import tpu as pltpu

F32 = jnp.float32
BF16 = jnp.bfloat16

GRID_W = 64
POOL_WINDOWS = (2, 4, 8, 16)
EPS = 1e-6
NEG_INF = -1e30
LOG2E = float(np.log2(np.e))

LANES = 128
V7X_VMEM_LIMIT_BYTES = 60 * 1024 * 1024

NORM_ROWS = 16

NA_QROWS = 4
NA_KROWS = 12
POOL_TILE = 256
POOL_HALO = 128
POOL_BLOCK_ROWS = 2048
FOURIER_BLOCK_ROWS = 1024
FOURIER_REV_ROWS = 256
FOURIER_FOLD_MIN_SEQ = 1024
CONV_TILE = 256
CONV_HALO = 16
CONV_ROWS = 64


def _cparams(*sem):
    return pltpu.CompilerParams(dimension_semantics=sem, vmem_limit_bytes=V7X_VMEM_LIMIT_BYTES)


def _pick(n, pref, unit=LANES):
    if n <= pref:
        return n
    t = (pref // unit) * unit
    while t > unit and n % t:
        t -= unit
    assert n % t == 0, (n, pref, unit)
    return t


def _sigmoid(x):
    return 0.5 * jnp.tanh(0.5 * x) + 0.5


def _silu(x):
    return x * _sigmoid(x)


def _rms(x, g):
    return x * lax.rsqrt(jnp.mean(x * x, axis=-1, keepdims=True) + EPS) * g


def _adaln_kernel(cond_ref, w_ref, b_ref, o_ref):
    s = _silu(cond_ref[...]).astype(BF16)
    o_ref[0] = jnp.dot(s, w_ref[0].astype(BF16), preferred_element_type=F32) + b_ref[0]


def _adaln(cond, w_ada, b_ada):
    depth, d, n = w_ada.shape
    r = cond.shape[0]
    tn = _pick(n, 1024)
    return pl.pallas_call(
        _adaln_kernel,
        grid=(depth, n // tn),
        in_specs=[pl.BlockSpec((r, d), lambda l, j: (0, 0)),
                  pl.BlockSpec((1, d, tn), lambda l, j: (l, 0, j)),
                  pl.BlockSpec((1, 1, tn), lambda l, j: (l, 0, j))],
        out_specs=pl.BlockSpec((1, r, tn), lambda l, j: (l, 0, j)),
        out_shape=jax.ShapeDtypeStruct((depth, r, n), F32),
        compiler_params=_cparams("parallel", "parallel"),
    )(cond, w_ada, b_ada.reshape(depth, 1, n))


def _norm_mod_kernel(x_ref, g_ref, shift_ref, scale_ref, o_ref, gain_scr, shift_scr):
    gain_scr[...] = jnp.broadcast_to(g_ref[...] * (1.0 + scale_ref[0]), gain_scr.shape)
    shift_scr[...] = jnp.broadcast_to(shift_ref[0], shift_scr.shape)

    def rows16(r, carry):
        rows = pl.ds(pl.multiple_of(r * NORM_ROWS, NORM_ROWS), NORM_ROWS)
        x = x_ref[rows, :]
        rstd = lax.rsqrt(jnp.mean(x * x, axis=-1, keepdims=True) + EPS)
        o_ref[rows, :] = (x * rstd * gain_scr[...] + shift_scr[...]).astype(o_ref.dtype)
        return carry

    lax.fori_loop(0, x_ref.shape[0] // NORM_ROWS, rows16, 0, unroll=4)


def _norm_mod(x, g, mod3, row_of):
    t, d = x.shape
    tm = _pick(t, 512, 8)
    return pl.pallas_call(
        _norm_mod_kernel,
        grid=(t // tm,),
        in_specs=[pl.BlockSpec((tm, d), lambda i: (i, 0)),
                  pl.BlockSpec((1, d), lambda i: (0, 0)),
                  pl.BlockSpec((1, 1, d), lambda i: (row_of(i * tm), 0, 0)),
                  pl.BlockSpec((1, 1, d), lambda i: (row_of(i * tm), 0, 1))],
        out_specs=pl.BlockSpec((tm, d), lambda i: (i, 0)),
        out_shape=jax.ShapeDtypeStruct((t, d), BF16),
        scratch_shapes=[pltpu.VMEM((NORM_ROWS, d), F32), pltpu.VMEM((NORM_ROWS, d), F32)],
        compiler_params=_cparams("parallel"),
    )(x, g.reshape(1, d), mod3, mod3)


def _side_casts(side, gi, gj):
    specs, shapes = [], []
    for s in side:
        rows, cols = s.shape
        nblk = min(gi * gj, rows // 16)
        while rows % nblk or (rows // nblk) % 16:
            nblk -= 1
        specs.append(pl.BlockSpec((rows // nblk, cols),
                                  lambda i, j, nblk=nblk: (jnp.minimum(i * gj + j, nblk - 1), 0)))
        shapes.append(jax.ShapeDtypeStruct(s.shape, BF16))
    return specs, shapes


def _cast_side(side_refs):
    n = len(side_refs) // 2
    for src_ref, dst_ref in zip(side_refs[:n], side_refs[n:]):
        dst_ref[...] = src_ref[...].astype(dst_ref.dtype)


def _matmul_kernel(a_ref, w_ref, *refs):
    n_side = (len(refs) - 1) // 2
    o_ref = refs[n_side]
    o_ref[...] = jnp.dot(a_ref[...], w_ref[...], preferred_element_type=F32).astype(o_ref.dtype)
    _cast_side(refs[:n_side] + refs[n_side + 1:])


def _matmul(a, w, side=()):
    m, k = a.shape
    n = w.shape[1]
    bm, bn = _pick(m, 1024, 8), _pick(n, 1024)
    gi, gj = m // bm, n // bn
    side_specs, side_shapes = _side_casts(side, gi, gj)
    outs = pl.pallas_call(
        _matmul_kernel,
        grid=(gi, gj),
        in_specs=[pl.BlockSpec((bm, k), lambda i, j: (i, 0)),
                  pl.BlockSpec((k, bn), lambda i, j: (0, j))] + side_specs,
        out_specs=[pl.BlockSpec((bm, bn), lambda i, j: (i, j))] + side_specs,
        out_shape=[jax.ShapeDtypeStruct((m, n), BF16)] + side_shapes,
        compiler_params=_cparams("arbitrary", "arbitrary"),
    )(a, w, *side)
    return outs[0], list(outs[1:])


def _out_proj_kernel(a1_ref, a2_ref, w1_ref, w2_ref, x_ref, gate_ref, *refs):
    n_side = (len(refs) - 1) // 2
    o_ref = refs[n_side]
    acc = jnp.dot(a1_ref[...], w1_ref[...], preferred_element_type=F32)
    acc = acc + jnp.dot(a2_ref[...], w2_ref[...], preferred_element_type=F32)
    o_ref[...] = x_ref[...] + gate_ref[0] * acc
    _cast_side(refs[:n_side] + refs[n_side + 1:])


def _out_proj(a1, a2, w, x, mod3, row_of, side=()):
    t, k1 = a1.shape
    d = w.shape[1]
    assert a2.shape == (t, k1) and w.shape[0] == 2 * k1
    bm, bn = _pick(t, 1024, 8), _pick(d, 512 if side else 1024)
    gi, gj = t // bm, d // bn
    gate_blk = 2 * gj
    side_specs, side_shapes = _side_casts(side, gi, gj)
    outs = pl.pallas_call(
        _out_proj_kernel,
        grid=(gi, gj),
        in_specs=[pl.BlockSpec((bm, k1), lambda i, j: (i, 0)),
                  pl.BlockSpec((bm, k1), lambda i, j: (i, 0)),
                  pl.BlockSpec((k1, bn), lambda i, j: (0, j)),
                  pl.BlockSpec((k1, bn), lambda i, j: (1, j)),
                  pl.BlockSpec((bm, bn), lambda i, j: (i, j)),
                  pl.BlockSpec((1, 1, bn), lambda i, j: (row_of(i * bm), 0, gate_blk + j))] + side_specs,
        out_specs=[pl.BlockSpec((bm, bn), lambda i, j: (i, j))] + side_specs,
        out_shape=[jax.ShapeDtypeStruct((t, d), F32)] + side_shapes,
        compiler_params=_cparams("arbitrary", "arbitrary"),
    )(a1, a2, w, w, x, mod3, *side)
    return outs[0], list(outs[1:])


def _dft_cos_sin(n):
    idx = np.arange(n, dtype=np.int64)
    ang = 2.0 * np.pi * ((idx[:, None] * idx[None, :]) % n) / n
    return np.cos(ang) / np.sqrt(n), np.sin(ang) / np.sqrt(n)


def _dft_fold_kernel(c_ref, s_ref, wf_ref, o_ref):
    cg = wf_ref.shape[1]
    hi = lax.Precision.HIGHEST
    o_ref[0, :, 0:cg] = jnp.dot(c_ref[...], wf_ref[0], precision=hi, preferred_element_type=F32).astype(BF16)
    o_ref[0, :, cg:] = jnp.dot(s_ref[...], wf_ref[0], precision=hi, preferred_element_type=F32).astype(BF16)


def _dft_fold(wf):
    groups, cg, _ = wf.shape
    cc, sc = _dft_cos_sin(cg)
    const = pl.BlockSpec((cg, cg), lambda g: (0, 0))
    return pl.pallas_call(
        _dft_fold_kernel,
        grid=(groups,),
        in_specs=[const, const, pl.BlockSpec((1, cg, cg), lambda g: (g, 0, 0))],
        out_specs=pl.BlockSpec((1, cg, 2 * cg), lambda g: (g, 0, 0)),
        out_shape=jax.ShapeDtypeStruct((groups, cg, 2 * cg), BF16),
        compiler_params=_cparams("parallel"),
    )(jnp.asarray(cc, F32), jnp.asarray(sc, F32), wf)


def _fourier_kernel(u_ref, gp_ref, cw_ref, csl_ref, jsh_ref, o_ref, q_scr, mid_scr, *, seq, cg, fold):
    if not fold:
        @pl.when(pl.program_id(2) == 0)
        def _():
            for s in range(u_ref.shape[0] // seq):
                p = jnp.dot(u_ref[s * seq:(s + 1) * seq, :], cw_ref[0], preferred_element_type=F32)
                q_scr[s, 0:seq, :] = p[:, :cg].astype(BF16)
                q_scr[s, seq:2 * seq, :] = p[:, cg:].astype(BF16)

        for s in range(u_ref.shape[0] // seq):
            rows = slice(s * csl_ref.shape[0], (s + 1) * csl_ref.shape[0])
            y = jnp.dot(csl_ref[...], q_scr[s], preferred_element_type=F32)
            o_ref[rows, :] = (y * _silu(gp_ref[rows, :].astype(F32))).astype(o_ref.dtype)
        return

    half = seq // 2
    rb = jsh_ref.shape[0]
    nblk = half // rb
    n_seq = u_ref.shape[0] // seq
    tr = csl_ref.shape[0]

    @pl.when(pl.program_id(2) == 0)
    def _():
        row0 = lax.broadcasted_iota(jnp.int32, (rb, 2 * cg), 0) == 0
        sign = jnp.where(lax.broadcasted_iota(jnp.int32, (1, 2 * cg), 1) < cg, 1.0, -1.0)
        for s in range(n_seq):
            p = jnp.dot(u_ref[s * seq:(s + 1) * seq, :], cw_ref[0], preferred_element_type=F32)
            mid_scr[s] = p[half:half + 8, :cg]
            upper = p[half:, :].astype(BF16)
            for a in range(nblk):
                rev = jnp.dot(jsh_ref[...], upper[(nblk - 1 - a) * rb:(nblk - a) * rb, :],
                              preferred_element_type=F32)
                if a > 0:
                    first = half + (nblk - a) * rb
                    rev = jnp.where(row0, p[first:first + 1, :], rev)
                q = p[a * rb:(a + 1) * rb, :] + sign * rev
                q_scr[s, a * rb:(a + 1) * rb, :] = q[:, :cg].astype(BF16)
                q_scr[s, half + a * rb:half + (a + 1) * rb, :] = q[:, cg:].astype(BF16)

    odd_row = (lax.broadcasted_iota(jnp.int32, (tr, 1), 0) & 1) == 1
    cos_mid = jnp.where(odd_row, -1.0, 1.0) * seq ** -0.5
    for s in range(n_seq):
        rows = slice(s * tr, (s + 1) * tr)
        y = jnp.dot(csl_ref[...], q_scr[s], preferred_element_type=F32) + cos_mid * mid_scr[s, 0:1, :]
        o_ref[rows, :] = (y * _silu(gp_ref[rows, :].astype(F32))).astype(o_ref.dtype)


def _fourier(u, cw, nb, seq, gp_col0):
    groups, cg, _ = cw.shape
    t = u.shape[0]
    tr = _pick(seq, 1024, 8)
    rt = seq // tr
    spb = 1
    if rt == 1:
        spb = max(1, min(nb, FOURIER_BLOCK_ROWS // seq))
        while nb % spb:
            spb -= 1
    fold = seq >= FOURIER_FOLD_MIN_SEQ
    half = seq // 2
    assert tr % 2 == 0 and half % 16 == 0
    cl, sl = _dft_cos_sin(seq)
    kpos = half if fold else seq
    csl = jnp.asarray(np.concatenate([cl[:, :kpos], -sl[:, :kpos]], axis=1), BF16)
    rb = _pick(half, FOURIER_REV_ROWS, 16)
    jsh = np.zeros((rb, rb), np.float32)
    jsh[np.arange(1, rb), rb - np.arange(1, rb)] = 1.0
    gp_blk0 = gp_col0 // cg
    return pl.pallas_call(
        functools.partial(_fourier_kernel, seq=seq, cg=cg, fold=fold),
        grid=(nb // spb, groups, rt),
        in_specs=[pl.BlockSpec((spb * seq, cg), lambda b, g, r: (b, g)),
                  pl.BlockSpec((spb * tr, cg), lambda b, g, r: (b * rt + r, gp_blk0 + g)),
                  pl.BlockSpec((1, cg, 2 * cg), lambda b, g, r: (g, 0, 0)),
                  pl.BlockSpec((tr, 2 * kpos), lambda b, g, r: (r, 0)),
                  pl.BlockSpec((rb, rb), lambda b, g, r: (0, 0))],
        out_specs=pl.BlockSpec((spb * tr, cg), lambda b, g, r: (b * rt + r, g)),
        out_shape=jax.ShapeDtypeStruct((t, groups * cg), BF16),
        scratch_shapes=[pltpu.VMEM((spb, 2 * kpos, cg), BF16), pltpu.VMEM((spb, 8, cg), F32)],
        compiler_params=_cparams("parallel", "parallel", "arbitrary"),
    )(u, u, cw, csl, jnp.asarray(jsh, BF16))


def _pool_tables(seq):
    r = np.arange(POOL_TILE)[:, None]
    s = np.arange(POOL_TILE + 2 * POOL_HALO)[None, :] - POOL_HALO
    tpos = np.arange(seq)
    bands, cnts = [], []
    for win in POOL_WINDOWS:
        half = win // 2
        assert half <= POOL_HALO
        bands.append(((s >= r - half) & (s <= r + half - 1)).astype(np.float32))
        cnts.append(np.minimum(tpos + half, seq) - np.maximum(tpos - half, 0))
    return np.stack(bands), np.stack(cnts).astype(np.float32)[:, :, None]


def _pool_kernel(u_ref, gp_ref, band_ref, cnt_ref, w_ref, ps_ref, o_ref, *, seq):
    tr = POOL_TILE
    for s0 in range(0, u_ref.shape[0], seq):
        for t in range(seq // tr):
            base = tr * t - POOL_HALO
            lo, hi = max(base, 0), min(base + tr + 2 * POOL_HALO, seq)
            rows = slice(s0 + tr * t, s0 + tr * (t + 1))
            wsum = jnp.dot(band_ref[0, :, lo - base:hi - base], u_ref[s0 + lo:s0 + hi, :],
                           preferred_element_type=F32)
            cnt = cnt_ref[0, tr * t:tr * (t + 1), :]
            d = (wsum / cnt - u_ref[rows, :].astype(F32)).astype(BF16)
            y = jnp.dot(d, w_ref[0], preferred_element_type=F32) * ps_ref[...]
            o_ref[rows, :] = (y * _silu(gp_ref[rows, :].astype(F32))).astype(o_ref.dtype)


def _pool(u, wp, pool_scale, nb, seq, in_col0, gp_col0):
    groups, cg, _ = wp.shape
    assert groups == len(POOL_WINDOWS) and seq % POOL_TILE == 0
    t = u.shape[0]
    band, cnt = _pool_tables(seq)
    in_blk0, gp_blk0 = in_col0 // cg, gp_col0 // cg
    spb = max(1, min(nb, POOL_BLOCK_ROWS // seq))
    while nb % spb:
        spb -= 1
    rows = spb * seq
    return pl.pallas_call(
        functools.partial(_pool_kernel, seq=seq),
        grid=(nb // spb, groups),
        in_specs=[pl.BlockSpec((rows, cg), lambda b, g: (b, in_blk0 + g)),
                  pl.BlockSpec((rows, cg), lambda b, g: (b, gp_blk0 + g)),
                  pl.BlockSpec((1,) + band.shape[1:], lambda b, g: (g, 0, 0)),
                  pl.BlockSpec((1, seq, 1), lambda b, g: (g, 0, 0)),
                  pl.BlockSpec((1, cg, cg), lambda b, g: (g, 0, 0)),
                  pl.BlockSpec((1, cg), lambda b, g: (0, g))],
        out_specs=pl.BlockSpec((rows, cg), lambda b, g: (b, g)),
        out_shape=jax.ShapeDtypeStruct((t, groups * cg), BF16),
        compiler_params=_cparams("parallel", "parallel"),
    )(u, u, jnp.asarray(band, BF16), jnp.asarray(cnt, F32), wp, pool_scale.reshape(1, groups * cg))


def _ctx_attn_kernel(q_ref, k_ref, v_ref, gp_ref, qg_ref, kg_ref, o_ref, sk_ref, sv_ref, *, heads, hd):
    qscale = hd ** -0.5 * LOG2E
    for h in range(heads):
        cols = slice(h * hd, (h + 1) * hd)
        qn = _rms(q_ref[:, cols].astype(F32), qg_ref[...]) * qscale
        kn = _rms(k_ref[:, cols].astype(F32), kg_ref[...])
        v = v_ref[:, cols]
        sk_ref[0, 0, h] = kn
        sv_ref[0, 0, h] = v.astype(F32)
        s = lax.dot_general(qn.astype(BF16), kn.astype(BF16), (((1,), (1,)), ((), ())),
                            preferred_element_type=F32)
        p = jnp.exp2(s - jnp.max(s, axis=-1, keepdims=True))
        v_ones = jnp.concatenate([v, jnp.ones_like(v)], axis=1)
        o = jnp.dot(p.astype(BF16), v_ones, preferred_element_type=F32)
        o = o[:, :hd] / o[:, hd:]
        o_ref[:, cols] = (o * _silu(gp_ref[:, cols].astype(F32))).astype(o_ref.dtype)


def _ctx_attn(u, qg, kg, nb, seq, d_attn, gp_col0):
    hd = qg.shape[-1]
    heads = d_attn // hd
    t = u.shape[0]
    state = jax.ShapeDtypeStruct((nb, 1, heads, seq, hd), F32)
    state_spec = pl.BlockSpec((1, 1, heads, seq, hd), lambda b: (b, 0, 0, 0, 0))
    gp_blk = gp_col0 // d_attn
    return pl.pallas_call(
        functools.partial(_ctx_attn_kernel, heads=heads, hd=hd),
        grid=(nb,),
        in_specs=[pl.BlockSpec((seq, d_attn), lambda b: (b, 0)),
                  pl.BlockSpec((seq, d_attn), lambda b: (b, 1)),
                  pl.BlockSpec((seq, d_attn), lambda b: (b, 2)),
                  pl.BlockSpec((seq, d_attn), lambda b: (b, gp_blk)),
                  pl.BlockSpec((1, hd), lambda b: (0, 0)),
                  pl.BlockSpec((1, hd), lambda b: (0, 0))],
        out_specs=[pl.BlockSpec((seq, d_attn), lambda b: (b, 0)), state_spec, state_spec],
        out_shape=[jax.ShapeDtypeStruct((t, d_attn), BF16), state, state],
        compiler_params=_cparams("parallel"),
    )(u, u, u, u, qg.reshape(1, hd), kg.reshape(1, hd))


def _nbr_plan(rows, kh, n_kh):
    assert rows % NA_QROWS == 0 and rows >= NA_KROWS
    n_dr = 2 * n_kh - 1
    blocks, types = [], []
    for r0 in range(0, rows, NA_QROWS):
        ws = int(np.clip(r0 - kh // 2, 0, rows - NA_KROWS))
        idx = np.full((NA_QROWS, NA_KROWS), n_dr, np.int32)
        for i in range(NA_QROWS):
            r = r0 + i
            start = int(np.clip(r - kh // 2, 0, rows - kh))
            assert ws <= start and start + kh <= ws + NA_KROWS
            for j in range(NA_KROWS):
                rk = ws + j
                if start <= rk < start + kh:
                    idx[i, j] = rk - r + n_kh - 1
        for t, known in enumerate(types):
            if np.array_equal(known, idx):
                break
        else:
            t = len(types)
            types.append(idx)
        blocks.append((r0, ws, t))
    return tuple(blocks), np.stack(types)


def _nbr_build_bias(rb_ref, bias_scr, head, blk_idx, n_dr, n_dc, kw):
    w = GRID_W
    c = lax.broadcasted_iota(jnp.int32, (w, 2 * w), 0)
    lane = lax.broadcasted_iota(jnp.int32, (w, 2 * w), 1)
    cp = lane % w
    q_start = jnp.clip(c - kw // 2, 0, w - kw)
    diff = jnp.where(cp >= q_start, jnp.where(cp < q_start + kw, cp - c + (kw - 1), -1), -1)
    neg = jnp.full((w, 2 * w), NEG_INF, F32)
    pairs = []
    for dr in range(n_dr):
        blk = neg
        for d in range(n_dc):
            blk = jnp.where(diff == d, rb_ref[(head * n_dr + dr) * n_dc + d] * LOG2E, blk)
        pairs.append(blk)
    pairs.append(neg)
    low = lane < w
    nt, qr, kr = blk_idx.shape
    for t in range(nt):
        for i in range(qr):
            for j in range(0, kr, 2):
                e, o = int(blk_idx[t, i, j]), int(blk_idx[t, i, j + 1])
                blk = pairs[e] if e == o else jnp.where(low, pairs[e], pairs[o])
                bias_scr[t, i * w:(i + 1) * w, j * w:(j + 2) * w] = blk


def _nbr_attn_kernel(rb_ref, q_ref, k_ref, v_ref, ck_ref, cv_ref, gp_ref, qg_ref, kg_ref, o_ref,
                     qn_scr, kn_scr, v1_scr, bias_scr, *, blocks, blk_idx, hd, kw, n_dr, n_dc):
    @pl.when(pl.program_id(1) == 0)
    def _():
        _nbr_build_bias(rb_ref, bias_scr, pl.program_id(0), blk_idx, n_dr, n_dc, kw)

    qscale = hd ** -0.5 * LOG2E
    qb, kb = NA_QROWS * GRID_W, NA_KROWS * GRID_W
    nt_dims = (((1,), (1,)), ((), ()))
    qn_scr[...] = (_rms(q_ref[...].astype(F32), qg_ref[...]) * qscale).astype(BF16)
    kn_scr[...] = _rms(k_ref[...].astype(F32), kg_ref[...]).astype(BF16)
    ck = ck_ref[0, 0, 0].astype(BF16)
    cv = cv_ref[0, 0, 0].astype(BF16)
    cv1 = jnp.concatenate([cv, jnp.ones_like(cv)], axis=1)
    v1_scr[:, 0:hd] = v_ref[...]
    v1_scr[:, hd:] = jnp.ones(v_ref.shape, BF16)

    def logits(n):
        r0, ws, t = blocks[n]
        q = qn_scr[r0 * GRID_W:r0 * GRID_W + qb, :]
        s_loc = lax.dot_general(q, kn_scr[ws * GRID_W:ws * GRID_W + kb, :], nt_dims,
                                preferred_element_type=F32) + bias_scr[t]
        return s_loc, lax.dot_general(q, ck, nt_dims, preferred_element_type=F32)

    def softmax(s_loc, s_ctx):
        m = jnp.maximum(jnp.max(s_loc, axis=-1, keepdims=True), jnp.max(s_ctx, axis=-1, keepdims=True))
        p_loc = jnp.exp2(s_loc - m)
        p_ctx = jnp.exp2(s_ctx - m)
        return p_loc.astype(BF16), p_ctx.astype(BF16)

    def attend(n, p_loc, p_ctx):
        r0, ws, _ = blocks[n]
        qs = slice(r0 * GRID_W, r0 * GRID_W + qb)
        o = jnp.dot(p_loc, v1_scr[ws * GRID_W:ws * GRID_W + kb, :], preferred_element_type=F32)
        o = o + jnp.dot(p_ctx, cv1, preferred_element_type=F32)
        o = o[:, :hd] / o[:, hd:]
        o_ref[qs, :] = (o * _silu(gp_ref[qs, :].astype(F32))).astype(o_ref.dtype)

    s_cur, p_prev = logits(0), None
    for n in range(len(blocks)):
        p_cur = softmax(*s_cur)
        if p_prev is not None:
            attend(n - 1, *p_prev)
        if n + 1 < len(blocks):
            s_cur = logits(n + 1)
        p_prev = p_cur
    attend(len(blocks) - 1, *p_prev)


def _nbr_attn(u, cache_k, cache_v, layer, rel_bias, qg, kg, nb, seq, d_attn, gp_col0):
    hd = qg.shape[-1]
    heads = d_attn // hd
    t = u.shape[0]
    past = cache_k.shape[3]
    rows = seq // GRID_W
    _, n_dr, n_dc = rel_bias.shape
    n_kh, kw = (n_dr + 1) // 2, (n_dc + 1) // 2
    assert 2 * GRID_W == LANES and kw <= GRID_W
    blocks, blk_idx = _nbr_plan(rows, min(n_kh, rows), n_kh)
    gp_blk0 = gp_col0 // hd
    cache_spec = pl.BlockSpec((1, 1, 1, past, hd), lambda h, b: (b, layer, h, 0, 0))
    return pl.pallas_call(
        functools.partial(_nbr_attn_kernel, blocks=blocks, blk_idx=blk_idx, hd=hd, kw=kw, n_dr=n_dr,
                          n_dc=n_dc),
        grid=(heads, nb),
        in_specs=[pl.BlockSpec(memory_space=pltpu.SMEM),
                  pl.BlockSpec((seq, hd), lambda h, b: (b, h)),
                  pl.BlockSpec((seq, hd), lambda h, b: (b, heads + h)),
                  pl.BlockSpec((seq, hd), lambda h, b: (b, 2 * heads + h)),
                  cache_spec, cache_spec,
                  pl.BlockSpec((seq, hd), lambda h, b: (b, gp_blk0 + h)),
                  pl.BlockSpec((1, hd), lambda h, b: (0, 0)),
                  pl.BlockSpec((1, hd), lambda h, b: (0, 0))],
        out_specs=pl.BlockSpec((seq, hd), lambda h, b: (b, h)),
        out_shape=jax.ShapeDtypeStruct((t, d_attn), BF16),
        scratch_shapes=[pltpu.VMEM((seq, hd), BF16), pltpu.VMEM((seq, hd), BF16),
                        pltpu.VMEM((seq, 2 * hd), BF16),
                        pltpu.VMEM((blk_idx.shape[0], NA_QROWS * GRID_W, NA_KROWS * GRID_W), F32)],
        compiler_params=_cparams("arbitrary", "arbitrary"),
    )(rel_bias.reshape(-1), u, u, u, cache_k, cache_v, u, qg.reshape(1, hd), kg.reshape(1, hd))


def _conv_kernel(a_ref, b_ref, ap_ref, bp_ref, an_ref, bn_ref, gp_ref, dw_ref, dwb_ref, lng_ref, lnb_ref,
                 wpw_ref, o_ref, h_scr, c_scr, z_scr, *, tiles_per_seq, width):
    tm, dc = a_ref.shape
    nc = dc // LANES
    pos = pl.program_id(0) % tiles_per_seq
    keep_prev = (pos > 0).astype(F32)
    keep_next = (pos < tiles_per_seq - 1).astype(F32)
    for c in range(nc):
        cols = slice(c * LANES, (c + 1) * LANES)
        h_scr[c, 0:CONV_HALO, :] = keep_prev * (
            ap_ref[:, cols].astype(F32) * _sigmoid(bp_ref[:, cols].astype(F32)))
        h_scr[c, CONV_HALO:CONV_HALO + tm, :] = (
            a_ref[:, cols].astype(F32) * _sigmoid(b_ref[:, cols].astype(F32)))
        h_scr[c, CONV_HALO + tm:, :] = keep_next * (
            an_ref[:, cols].astype(F32) * _sigmoid(bn_ref[:, cols].astype(F32)))

    off = CONV_HALO - width // 2

    def chunk(c, carry):
        for rb in range(tm // CONV_ROWS):
            acc = jnp.zeros((CONV_ROWS, LANES), F32)
            for k in range(width):
                acc = acc + dw_ref[c, k:k + 1, :] * h_scr[c, pl.ds(rb * CONV_ROWS + off + k, CONV_ROWS), :]
            c_scr[c, rb * CONV_ROWS:(rb + 1) * CONV_ROWS, :] = acc + dwb_ref[c]
        return carry

    lax.fori_loop(0, nc, chunk, 0)

    s1 = jnp.zeros((tm, LANES), F32)
    for c in range(nc):
        s1 = s1 + c_scr[c]
    mu = jnp.sum(s1, axis=-1, keepdims=True) / dc
    s2 = jnp.zeros((tm, LANES), F32)
    for c in range(nc):
        xc = c_scr[c] - mu
        s2 = s2 + xc * xc
    rstd = lax.rsqrt(jnp.sum(s2, axis=-1, keepdims=True) / dc + EPS)
    for c in range(nc):
        cols = slice(c * LANES, (c + 1) * LANES)
        y = (c_scr[c] - mu) * rstd * lng_ref[:, cols] + lnb_ref[:, cols]
        z_scr[:, cols] = _silu(y).astype(BF16)
    out = jnp.dot(z_scr[...], wpw_ref[...], preferred_element_type=F32)
    o_ref[...] = (out * _silu(gp_ref[...].astype(F32))).astype(o_ref.dtype)


def _conv_module(u, dw, dw_b, ln_g, ln_b, w_pw, seq, in_col0, gp_col0):
    width, dc = dw.shape
    t = u.shape[0]
    tm = CONV_TILE
    assert seq % tm == 0 and width // 2 <= CONV_HALO and tm % CONV_HALO == 0 and dc % LANES == 0
    nc = dc // LANES
    tiles_per_seq = seq // tm
    hb = tm // CONV_HALO
    n_hblk = t // CONV_HALO
    a_blk, gp_blk = in_col0 // dc, gp_col0 // dc
    wpad = -width % 8
    dw3 = jnp.pad(dw, ((0, wpad), (0, 0))).reshape(width + wpad, nc, LANES).transpose(1, 0, 2)
    prev = lambda i: jnp.maximum(i * hb - 1, 0)
    nxt = lambda i: jnp.minimum((i + 1) * hb, n_hblk - 1)
    vec = lambda x: x.reshape(1, dc)
    return pl.pallas_call(
        functools.partial(_conv_kernel, tiles_per_seq=tiles_per_seq, width=width),
        grid=(t // tm,),
        in_specs=[pl.BlockSpec((tm, dc), lambda i: (i, a_blk)),
                  pl.BlockSpec((tm, dc), lambda i: (i, a_blk + 1)),
                  pl.BlockSpec((CONV_HALO, dc), lambda i: (prev(i), a_blk)),
                  pl.BlockSpec((CONV_HALO, dc), lambda i: (prev(i), a_blk + 1)),
                  pl.BlockSpec((CONV_HALO, dc), lambda i: (nxt(i), a_blk)),
                  pl.BlockSpec((CONV_HALO, dc), lambda i: (nxt(i), a_blk + 1)),
                  pl.BlockSpec((tm, dc), lambda i: (i, gp_blk)),
                  pl.BlockSpec((nc, width + wpad, LANES), lambda i: (0, 0, 0)),
                  pl.BlockSpec((nc, 1, LANES), lambda i: (0, 0, 0)),
                  pl.BlockSpec((1, dc), lambda i: (0, 0)),
                  pl.BlockSpec((1, dc), lambda i: (0, 0)),
                  pl.BlockSpec((dc, dc), lambda i: (0, 0))],
        out_specs=pl.BlockSpec((tm, dc), lambda i: (i, 0)),
        out_shape=jax.ShapeDtypeStruct((t, dc), BF16),
        scratch_shapes=[pltpu.VMEM((nc, tm + 2 * CONV_HALO, LANES), F32),
                        pltpu.VMEM((nc, tm, LANES), F32),
                        pltpu.VMEM((tm, dc), BF16)],
        compiler_params=_cparams("parallel"),
    )(u, u, u, u, u, u, u, dw3, dw_b.reshape(nc, 1, LANES), vec(ln_g), vec(ln_b), w_pw)


def _with_casts(fn, wanted, wts, bf16):
    todo = [key for key in wanted if key not in bf16]
    out, copies = fn(side=[wts[name][jj] for name, jj in todo])
    bf16.update(zip(todo, copies))
    return out


def _in_proj(h, i, wts, bf16):
    j = i // 2
    own = ("w_in_even", j) if i % 2 == 0 else ("w_in_odd", j)
    wanted = [("w_out_even", j)] if i % 2 == 0 else [("w_out_odd", j), ("w_conv_pw", j)]
    if i + 1 < wts["norm_g"].shape[0]:
        wanted.append(("w_in_odd" if i % 2 == 0 else "w_in_even", (i + 1) // 2))
    if own not in bf16:
        bf16[own] = wts[own[0]][own[1]].astype(BF16)
    return _with_casts(functools.partial(_matmul, h, bf16[own]), wanted, wts, bf16)


def _layer_out(y1, y2, x, mod3, row_of, i, wts, bf16):
    own = ("w_out_even", i // 2) if i % 2 == 0 else ("w_out_odd", i // 2)
    return _with_casts(functools.partial(_out_proj, y1, y2, bf16[own], x, mod3, row_of), [], wts, bf16)


def _trunk(x, nb, seq, row_of, mod, caches, wts, bf16):
    d = x.shape[1]
    depth = mod.shape[0]
    new_k, new_v = [], []
    for i in range(depth):
        mod3 = mod[i][:, None, :]
        h = _norm_mod(x, wts["norm_g"][i], mod3, row_of)
        j = i // 2
        if i % 2 == 0:
            u = _in_proj(h, i, wts, bf16)
            d_f = wts["w_fourier"].shape[1] * wts["w_fourier"].shape[2]
            d_inner = d_f + wts["w_pool"].shape[1] * wts["w_pool"].shape[2]
            y1 = _fourier(u, wts["dft_w"][j], nb, seq, gp_col0=d_inner)
            y2 = _pool(u, wts["w_pool"][j], wts["pool_scale"][j], nb, seq, in_col0=d_f,
                       gp_col0=d_inner + d_f)
            x = _layer_out(y1, y2, x, mod3, row_of, i, wts, bf16)
        else:
            u = _in_proj(h, i, wts, bf16)
            d_conv = wts["conv_dw"].shape[2]
            d_attn = (u.shape[1] - 3 * d_conv) // 4
            gp_col0 = 3 * d_attn + 2 * d_conv
            if caches is None:
                y1, sk, sv = _ctx_attn(u, wts["q_norm_g"][j], wts["k_norm_g"][j], nb, seq, d_attn, gp_col0)
                new_k.append(sk)
                new_v.append(sv)
            else:
                y1 = _nbr_attn(u, caches[0], caches[1], j, wts["rel_bias"][j], wts["q_norm_g"][j],
                               wts["k_norm_g"][j], nb, seq, d_attn, gp_col0)
            y2 = _conv_module(u, wts["conv_dw"][j], wts["conv_dw_b"][j], wts["conv_ln_g"][j],
                              wts["conv_ln_b"][j], bf16["w_conv_pw", j], seq, in_col0=3 * d_attn,
                              gp_col0=gp_col0 + d_attn)
            x = _layer_out(y1, y2, x, mod3, row_of, i, wts, bf16)
    return x, new_k, new_v


def kernel(x_prompt, x_sample, cache_k, cache_v, c, c_ctx, norm_g, w_ada, b_ada, w_in_even, w_out_even,
           w_fourier, w_pool, pool_scale, w_in_odd, w_out_odd, q_norm_g, k_norm_g, rel_bias, conv_dw,
           conv_dw_b, conv_ln_g, conv_ln_b, w_conv_pw):
    nb_p, seq_p, d = x_prompt.shape
    nb_s, seq_s, _ = x_sample.shape
    n_rows = -(-(1 + nb_s) // 8) * 8
    cond = jnp.concatenate([c_ctx[None, :], c, jnp.zeros((n_rows - 1 - nb_s, d), F32)], axis=0)
    mod = _adaln(cond, w_ada, b_ada)
    wts = dict(norm_g=norm_g, pool_scale=pool_scale, q_norm_g=q_norm_g, k_norm_g=k_norm_g,
               rel_bias=rel_bias, conv_dw=conv_dw, conv_dw_b=conv_dw_b, conv_ln_g=conv_ln_g,
               conv_ln_b=conv_ln_b, w_in_even=w_in_even, w_out_even=w_out_even, w_in_odd=w_in_odd,
               w_out_odd=w_out_odd, w_conv_pw=w_conv_pw, w_fourier=w_fourier,
               w_pool=w_pool.astype(BF16))
    wts["dft_w"] = [_dft_fold(w_fourier[j]) for j in range(w_fourier.shape[0])]
    bf16 = {}
    y_p, ks, vs = _trunk(x_prompt.reshape(nb_p * seq_p, d), nb_p, seq_p, lambda r: 0, mod, None, wts, bf16)
    y_s, _, _ = _trunk(x_sample.reshape(nb_s * seq_s, d), nb_s, seq_s, lambda r: 1 + r // seq_s, mod,
                       (cache_k, cache_v), wts, bf16)
    state_k = jnp.concatenate(ks, axis=1)
    state_v = jnp.concatenate(vs, axis=1)
    return (y_p.reshape(nb_p, seq_p, d), y_s.reshape(nb_s, seq_s, d), state_k, state_v)
```
